```python
import math, functools
import jax, jax.numpy as jnp
from jax import lax
import numpy as np

D_MODEL = 1024
BATCH = 32
SEQ = 2048
DEPTH = 2

GRID_W = 64
CTX_LEN = 256
HEAD_DIM = 64
D_MIX = D_MODEL
N_GROUPS = 4
GROUP_W = D_MIX // N_GROUPS
N_HEADS_G = GROUP_W // HEAD_DIM
CHUNK = 128
ROPE_PAIRS = HEAD_DIM // 4
ROPE_BASE = 10000.0
RWKV_W_LORA = 32
RWKV_A_LORA = 32
RWKV_G_LORA = 64
RWKV_SHIFT_W = 3 * GROUP_W + RWKV_W_LORA + RWKV_A_LORA + RWKV_G_LORA
RWKV_SPLIT = (GROUP_W, 2 * GROUP_W, 3 * GROUP_W, 3 * GROUP_W + RWKV_W_LORA, 3 * GROUP_W + RWKV_W_LORA + RWKV_A_LORA)
HYENA_EMB = 33
HYENA_BANDS = (HYENA_EMB - 1) // 2
HYENA_FFN = 64
HYENA_ORDER = 2
HYENA_FAST_DECAY = 0.3
HYENA_SLOW_DECAY = 1.5
HYENA_TARGET = 1e-2
D_FF = 2816
DEEPNORM_ALPHA = (2 * DEPTH) ** 0.25
DEEPNORM_BETA = (8 * DEPTH) ** -0.25
LN_EPS = 1e-5
GN_EPS = 1e-5
SPLIT_SIZES = (GROUP_W, GROUP_W, GROUP_W, GROUP_W, 4 * N_HEADS_G,
               GROUP_W, GROUP_W, GROUP_W,
               GROUP_W, GROUP_W, GROUP_W, GROUP_W,
               GROUP_W, GROUP_W, GROUP_W, RWKV_W_LORA, RWKV_A_LORA, RWKV_G_LORA)
D_PROJ = sum(SPLIT_SIZES)
F32 = jnp.float32

kernel_name = 'hybrid_mlstm_hyena_retnet_rwkv7_prefix_dit'


def layer_norm(x, g=None, b=None):
    xf = x.astype(F32)
    mu = jnp.mean(xf, -1, keepdims=True)
    var = jnp.mean(jnp.square(xf - mu), -1, keepdims=True)
    y = (xf - mu) * lax.rsqrt(var + LN_EPS)
    if g is not None:
        y = y * g + b
    return y.astype(x.dtype)


def modulate(u, shift, scale):
    return u * (1 + scale) + shift


def dwconv3(u, w, b):
    up = jnp.pad(u, ((0, 0), (1, 1), (0, 0)))
    return up[:, :-2] * w[0] + up[:, 1:-1] * w[1] + up[:, 2:] * w[2] + b


def token_shift(u, mu):
    up = jnp.pad(u, ((0, 0), (1, 1), (0, 0)))
    return u + mu[0] * (up[:, :-2] - u) + mu[1] * (up[:, 2:] - u)


def to_heads(u):
    B, L, W = u.shape
    return u.reshape(B, L, W // HEAD_DIM, HEAD_DIM).transpose(0, 2, 1, 3)


def from_heads(h):
    B, H, L, d = h.shape
    return h.transpose(0, 2, 1, 3).reshape(B, L, H * d)


def head_group_norm(y, g, b):
    B, L, W = y.shape
    yf = y.astype(F32).reshape(B, L, W // HEAD_DIM, HEAD_DIM)
    mu = jnp.mean(yf, -1, keepdims=True)
    var = jnp.mean(jnp.square(yf - mu), -1, keepdims=True)
    yf = ((yf - mu) * lax.rsqrt(var + GN_EPS)).reshape(B, L, W)
    return yf * g + b


def split_columns(proj):
    return jnp.split(proj, np.cumsum(SPLIT_SIZES)[:-1].tolist(), axis=-1)


def rope_2d(x, ang_row, ang_col):
    def rot(xh, a):
        c, s = jnp.cos(a), jnp.sin(a)
        x1, x2 = xh[..., :ROPE_PAIRS], xh[..., ROPE_PAIRS:]
        return jnp.concatenate([x1 * c - x2 * s, x2 * c + x1 * s], -1)
    half = HEAD_DIM // 2
    return jnp.concatenate([rot(x[..., :half], ang_row), rot(x[..., half:], ang_col)], -1)


def to_chunks(t):
    B, H, L = t.shape[:3]
    t = t.reshape((B, H, L // CHUNK, CHUNK) + t.shape[3:])
    return jnp.moveaxis(t, 2, 0)


def from_chunks(t):
    t = jnp.moveaxis(t, 0, 2)
    B, H, N, T = t.shape[:4]
    return t.reshape((B, H, N * T) + t.shape[4:])


def mlstm_chunk_scan(q, k, v, ig, lf, state):
    causal = jnp.tril(jnp.ones((CHUNK, CHUNK), dtype=bool))

    def step(carry, inp):
        c_mem, n_mem, m_mem = carry
        qc, kc, vc, igc, lfc = inp
        b = jnp.cumsum(lfc, axis=-1)
        dlog = jnp.where(causal, b[..., :, None] - b[..., None, :] + igc[..., None, :], -jnp.inf)
        inter = b + m_mem[..., None]
        m_t = jnp.maximum(inter, jnp.max(dlog, -1))
        s = jnp.einsum('bhid,bhjd->bhij', qc, kc) * jnp.exp(dlog - m_t[..., None])
        w_inter = jnp.exp(inter - m_t)
        num = jnp.einsum('bhij,bhjd->bhid', s, vc) + w_inter[..., None] * jnp.einsum('bhid,bhde->bhie', qc, c_mem)
        den = jnp.sum(s, -1) + w_inter * jnp.einsum('bhid,bhd->bhi', qc, n_mem)
        h = num / jnp.maximum(jnp.abs(den), jnp.exp(-m_t))[..., None]
        b_last = b[..., -1]
        g_log = b_last[..., None] - b + igc
        m_new = jnp.maximum(b_last + m_mem, jnp.max(g_log, -1))
        wk = jnp.exp(g_log - m_new[..., None])
        decay = jnp.exp(b_last + m_mem - m_new)
        c_new = decay[..., None, None] * c_mem + jnp.einsum('bhjd,bhje->bhde', kc * wk[..., None], vc)
        n_new = decay[..., None] * n_mem + jnp.einsum('bhjd,bhj->bhd', kc, wk)
        return (c_new, n_new, m_new), h

    final, h = lax.scan(step, state, tuple(to_chunks(t) for t in (q, k, v, ig, lf)))
    return from_chunks(h), final


def retention_chunk_scan(q, k, v, state, log_gamma):
    idx = jnp.arange(CHUNK, dtype=F32)
    diff = idx[:, None] - idx[None, :]
    causal = diff >= 0
    decay_mask = jnp.where(causal, jnp.exp(jnp.where(causal, diff, 0.0) * log_gamma[:, None, None]), 0.0)
    q_dec = jnp.exp((idx + 1.0) * log_gamma[:, None])
    k_dec = jnp.exp((CHUNK - 1.0 - idx) * log_gamma[:, None])
    chunk_dec = jnp.exp(CHUNK * log_gamma)

    def step(s_mem, inp):
        qc, kc, vc = inp
        s = jnp.einsum('bhid,bhjd->bhij', qc, kc) * decay_mask
        o = jnp.einsum('bhij,bhjd->bhid', s, vc) + jnp.einsum('bhid,bhde->bhie', qc * q_dec[..., None], s_mem)
        s_new = s_mem * chunk_dec[:, None, None] + jnp.einsum('bhjd,bhje->bhde', kc * k_dec[..., None], vc)
        return s_new, o

    final, o = lax.scan(step, state, tuple(to_chunks(t) for t in (q, k, v)))
    return from_chunks(o), final


def rwkv_scan(r, w, k, v, a, b, state):
    def step(s_mem, inp):
        rt, wt, kt, vt, at, bt = inp
        sa = jnp.einsum('bhvk,bhk->bhv', s_mem, at)
        s_new = s_mem * wt[:, :, None, :] + sa[..., None] * bt[:, :, None, :] + vt[..., None] * kt[:, :, None, :]
        return s_new, jnp.einsum('bhvk,bhk->bhv', s_new, rt)

    xs = tuple(jnp.moveaxis(t, 2, 0) for t in (r, w, k, v, a, b))
    final, y = lax.scan(step, state, xs)
    return jnp.moveaxis(y, 0, 2), final


def bidirectional_scan(scan_fw, scan_bw, ctx_fw, ctx_bw, lat_fw, lat_bw, init):
    flip = lambda ts: tuple(jnp.flip(t, axis=2) for t in ts)
    hc_f, st_f = scan_fw(*ctx_fw, init)
    hc_b, st_b = scan_bw(*flip(ctx_bw), init)
    h_f, _ = scan_fw(*lat_fw, st_f)
    h_b, _ = scan_bw(*flip(lat_bw), st_b)
    return h_f + jnp.flip(h_b, axis=2), hc_f + jnp.flip(hc_b, axis=2)


def mlstm_mixer(parts, parts_c, conv_w, conv_b, gate_b, ng, nb):
    def prep(q, k, v, o, gates):
        B, L, _ = q.shape
        qk = jax.nn.silu(dwconv3(jnp.concatenate([q, k], -1), conv_w, conv_b)).astype(F32)
        qh = to_heads(qk[..., :GROUP_W]) * HEAD_DIM ** -0.5
        kh = to_heads(qk[..., GROUP_W:])
        vh = to_heads(v.astype(F32))
        g = (gates.reshape(B, L, 4, N_HEADS_G) + gate_b).astype(F32).transpose(2, 0, 3, 1)
        fw = (qh, kh, vh, g[0], jax.nn.log_sigmoid(g[1]))
        bw = (qh, kh, vh, g[2], jax.nn.log_sigmoid(g[3]))
        return fw, bw

    lat_fw, lat_bw = prep(*parts)
    ctx_fw, ctx_bw = prep(*parts_c)
    B = parts[0].shape[0]
    init = (jnp.zeros((B, N_HEADS_G, HEAD_DIM, HEAD_DIM), F32),
            jnp.zeros((B, N_HEADS_G, HEAD_DIM), F32),
            jnp.zeros((B, N_HEADS_G), F32))
    h, h_c = bidirectional_scan(mlstm_chunk_scan, mlstm_chunk_scan, ctx_fw, ctx_bw, lat_fw, lat_bw, init)
    out = lambda o, hh: jax.nn.sigmoid(o.astype(F32)) * head_group_norm(from_heads(hh), ng, nb)
    return out(parts[3], h), out(parts_c[3], h_c)


def hyena_kernels(L, w1, b1, w2, b2, w3, freq):
    w1, b1, w2, b2, w3, freq = (t.astype(F32) for t in (w1, b1, w2, b2, w3, freq))
    pos = jnp.arange(L, dtype=F32)
    t = jnp.linspace(0.0, 1.0, L, dtype=F32)[:, None]
    ang = 2.0 * math.pi * pos[:, None] / L
    bands = jnp.linspace(1e-4, HYENA_BANDS - 1, HYENA_BANDS, dtype=F32)[None, :]
    feats = jnp.concatenate([t, jnp.cos(bands * ang), -jnp.sin(bands * ang)], -1)
    h = jnp.sin(freq * (feats @ w1 + b1))
    h = jnp.sin(freq * (h @ w2 + b2))
    h = h @ w3
    max_decay = math.log(HYENA_TARGET) / HYENA_FAST_DECAY
    min_decay = math.log(HYENA_TARGET) / HYENA_SLOW_DECAY
    deltas = jnp.abs(jnp.linspace(min_decay, max_decay, GROUP_W, dtype=F32))
    window = jnp.exp(-t * deltas)
    h = h.reshape(L, 2, HYENA_ORDER, GROUP_W) * window[:, None, None, :]
    hf, hb = h[:, 0], h[:, 1]
    return jnp.concatenate([hf[:1] + hb[:1], hf[1:], jnp.zeros_like(hf[:1]), hb[:0:-1]], 0)


def fftconv(u, k_full):
    L = u.shape[1]
    uf = jnp.fft.rfft(u, n=2 * L, axis=1)
    kf = jnp.fft.rfft(k_full, n=2 * L, axis=0)
    return jnp.fft.irfft(uf * kf[None], n=2 * L, axis=1)[:, :L]


def hyena_mixer(parts, parts_c, conv_w, conv_b, w1, b1, w2, b2, w3, freq, d_bias, ng, nb):
    def run(v, x1, x2):
        L = v.shape[1]
        k_full = hyena_kernels(L, w1, b1, w2, b2, w3, freq)
        z = dwconv3(jnp.concatenate([v, x1, x2], -1), conv_w, conv_b).astype(F32)
        v, x1, x2 = jnp.split(z, [GROUP_W, 2 * GROUP_W], axis=-1)
        z2 = x1 * (fftconv(v, k_full[:, 0]) + v * d_bias[0])
        y = x2 * (fftconv(z2, k_full[:, 1]) + z2 * d_bias[1])
        return head_group_norm(y, ng, nb)

    y = run(*parts)
    y_c = run(*parts_c) if parts_c is not None else None
    return y, y_c


def retention_mixer(parts, parts_c, ang, ng, nb):
    lg_f = jnp.log(1.0 - 2.0 ** (-5.0 - jnp.arange(N_HEADS_G, dtype=F32)))
    lg_b = lg_f[::-1]

    def prep(q, k, v, rotate):
        qh = to_heads(q.astype(F32)) * HEAD_DIM ** -0.5
        kh = to_heads(k.astype(F32))
        if rotate:
            qh, kh = rope_2d(qh, *ang), rope_2d(kh, *ang)
        return (qh, kh, to_heads(v.astype(F32)))

    lat = prep(*parts[:3], True)
    ctx = prep(*parts_c[:3], False)
    B = parts[0].shape[0]
    init = jnp.zeros((B, N_HEADS_G, HEAD_DIM, HEAD_DIM), F32)
    h, h_c = bidirectional_scan(functools.partial(retention_chunk_scan, log_gamma=lg_f),
                                functools.partial(retention_chunk_scan, log_gamma=lg_b),
                                ctx, ctx, lat, lat, init)
    out = lambda g, hh: jax.nn.silu(g.astype(F32)) * head_group_norm(from_heads(hh), ng, nb)
    return out(parts[3], h), out(parts_c[3], h_c)


def rwkv_mixer(parts, parts_c, mu, w0, w2, a0, a2, g2, kk_scale, ka, rk, ng, nb):
    def prep(seq_parts):
        z = token_shift(jnp.concatenate(seq_parts, -1).astype(F32), mu)
        r, k, v, wlo, alo, glo = jnp.split(z, list(RWKV_SPLIT), axis=-1)
        g = jax.nn.sigmoid(glo) @ g2
        rh, kh, vh = to_heads(r), to_heads(k), to_heads(v)
        kk = to_heads(k * kk_scale)
        kk = kk * lax.rsqrt(jnp.sum(kk * kk, -1, keepdims=True) + 1e-12)
        dirs = []
        for d in range(2):
            w_pre = w0[d] + jnp.tanh(wlo) @ w2[d]
            decay = jnp.exp(-jnp.exp(-jax.nn.softplus(-w_pre) - 0.5))
            a = jax.nn.sigmoid(a0[d] + alo @ a2[d])
            k_d = k * (1 + (a - 1) * ka)
            dirs.append((rh, to_heads(decay), to_heads(k_d), vh, -kk, kk * to_heads(a)))
        bonus = jnp.sum(rh * kh * rk[None, :, None, :], -1, keepdims=True) * vh
        return dirs[0], dirs[1], bonus, g

    lat_fw, lat_bw, bonus, g = prep(parts)
    ctx_fw, ctx_bw, bonus_c, g_c = prep(parts_c)
    B = parts[0].shape[0]
    init = jnp.zeros((B, N_HEADS_G, HEAD_DIM, HEAD_DIM), F32)
    h, h_c = bidirectional_scan(rwkv_scan, rwkv_scan, ctx_fw, ctx_bw, lat_fw, lat_bw, init)
    out = lambda hh, bon, gg: gg * (head_group_norm(from_heads(hh), ng, nb) + from_heads(bon))
    return out(h, bonus, g), out(h_c, bonus_c, g_c)


def token_mixer(u, uc, p, ang, need_ctx):
    parts = split_columns(u @ p['w_in'])
    parts_c = split_columns(uc @ p['w_in'])
    ng, nb = p['out_norm_g'], p['out_norm_b']
    gs = lambda i, t: t[i * GROUP_W:(i + 1) * GROUP_W]
    ya, ya_c = mlstm_mixer(parts[0:5], parts_c[0:5], p['mlstm_conv_w'], p['mlstm_conv_b'], p['mlstm_gate_b'],
                           gs(0, ng), gs(0, nb))
    yb, yb_c = hyena_mixer(parts[5:8], parts_c[5:8] if need_ctx else None, p['hyena_conv_w'], p['hyena_conv_b'],
                           p['hyena_w1'], p['hyena_b1'], p['hyena_w2'], p['hyena_b2'], p['hyena_w3'],
                           p['hyena_freq'], p['hyena_d'], gs(1, ng), gs(1, nb))
    yr, yr_c = retention_mixer(parts[8:12], parts_c[8:12], ang, gs(2, ng), gs(2, nb))
    yw, yw_c = rwkv_mixer(parts[12:18], parts_c[12:18], p['rwkv_mu'], p['rwkv_w0'], p['rwkv_w2'], p['rwkv_a0'],
                          p['rwkv_a2'], p['rwkv_g2'], p['rwkv_kk'], p['rwkv_ka'], p['rwkv_rk'], gs(3, ng), gs(3, nb))
    y = jnp.concatenate([ya, yb, yr, yw], -1).astype(u.dtype) @ p['w_out']
    if not need_ctx:
        return y, None
    y_c = jnp.concatenate([ya_c, yb_c, yr_c, yw_c], -1).astype(uc.dtype) @ p['w_out']
    return y, y_c


def conv_ffn(u, w_up, conv_w, conv_b, w_down):
    h = u @ w_up
    a, b = jnp.split(h, [D_FF], axis=-1)
    return (jax.nn.silu(dwconv3(a, conv_w, conv_b)) * b) @ w_down


def setup_inputs(seed: int = 0) -> dict:
    key = jax.random.key(seed)
    ks = iter(jax.random.split(key, 48))

    def nrm(shape, std):
        return std * jax.random.normal(next(ks), shape, F32)

    def conv_init(width):
        return nrm((DEPTH, 3, width), 0.3) + jnp.array([0.0, 1.0, 0.0], F32)[None, :, None]

    head_lin = jnp.linspace(3.0, 6.0, N_HEADS_G, dtype=F32)
    f_mask = jnp.array([0.0, 1.0, 0.0, 1.0], F32)[:, None]
    ratio = jnp.arange(GROUP_W, dtype=F32) / (GROUP_W - 1)
    decay_speed = -7.0 + 5.0 * ratio ** 1.35
    return {
        'x': nrm((BATCH, SEQ, D_MODEL), 1.0),
        'c': nrm((BATCH, D_MODEL), 1.0),
        'ctx': nrm((BATCH, CTX_LEN, D_MODEL), 1.0),
        'c_ctx': nrm((D_MODEL,), 1.0),
        'ada_w': nrm((DEPTH, D_MODEL, 6 * D_MODEL), 0.5 * D_MODEL ** -0.5),
        'ada_b': nrm((DEPTH, 6 * D_MODEL), 0.01),
        'w_in': nrm((DEPTH, D_MODEL, D_PROJ), D_MODEL ** -0.5),
        'mlstm_conv_w': conv_init(2 * GROUP_W),
        'mlstm_conv_b': nrm((DEPTH, 2 * GROUP_W), 0.01),
        'mlstm_gate_b': nrm((DEPTH, 4, N_HEADS_G), 0.1) + f_mask * head_lin[None, :],
        'hyena_conv_w': conv_init(3 * GROUP_W),
        'hyena_conv_b': nrm((DEPTH, 3 * GROUP_W), 0.01),
        'hyena_w1': nrm((DEPTH, HYENA_EMB, HYENA_FFN), HYENA_EMB ** -0.5),
        'hyena_b1': nrm((DEPTH, HYENA_FFN), 0.01),
        'hyena_w2': nrm((DEPTH, HYENA_FFN, HYENA_FFN), HYENA_FFN ** -0.5),
        'hyena_b2': nrm((DEPTH, HYENA_FFN), 0.01),
        'hyena_w3': nrm((DEPTH, HYENA_FFN, 2 * HYENA_ORDER * GROUP_W), HYENA_FFN ** -0.5),
        'hyena_freq': 1.0 + nrm((DEPTH, HYENA_FFN), 0.01),
        'hyena_d': nrm((DEPTH, HYENA_ORDER, GROUP_W), 0.5),
        'rwkv_mu': jax.random.uniform(next(ks), (DEPTH, 2, RWKV_SHIFT_W), F32, 0.0, 0.5),
        'rwkv_w0': decay_speed + 0.5 + nrm((DEPTH, 2, GROUP_W), 0.1),
        'rwkv_w2': nrm((DEPTH, 2, RWKV_W_LORA, GROUP_W), 0.1 * RWKV_W_LORA ** -0.5),
        'rwkv_a0': nrm((DEPTH, 2, GROUP_W), 0.1),
        'rwkv_a2': nrm((DEPTH, 2, RWKV_A_LORA, GROUP_W), 0.1 * RWKV_A_LORA ** -0.5),
        'rwkv_g2': nrm((DEPTH, RWKV_G_LORA, GROUP_W), RWKV_G_LORA ** -0.5),
        'rwkv_kk': 0.85 + nrm((DEPTH, GROUP_W), 0.02),
        'rwkv_ka': 1.0 + nrm((DEPTH, GROUP_W), 0.02),
        'rwkv_rk': nrm((DEPTH, N_HEADS_G, HEAD_DIM), 0.1),
        'out_norm_g': 1.0 + nrm((DEPTH, D_MIX), 0.02),
        'out_norm_b': nrm((DEPTH, D_MIX), 0.01),
        'w_out': nrm((DEPTH, D_MIX, D_MODEL), DEEPNORM_BETA * D_MIX ** -0.5),
        'ln1_g': 1.0 + nrm((DEPTH, D_MODEL), 0.02),
        'ln1_b': nrm((DEPTH, D_MODEL), 0.01),
        'ffn_w_up': nrm((DEPTH, D_MODEL, 2 * D_FF), D_MODEL ** -0.5),
        'ffn_conv_w': conv_init(D_FF),
        'ffn_conv_b': nrm((DEPTH, D_FF), 0.01),
        'ffn_w_down': nrm((DEPTH, D_FF, D_MODEL), DEEPNORM_BETA * D_FF ** -0.5),
        'ln2_g': 1.0 + nrm((DEPTH, D_MODEL), 0.02),
        'ln2_b': nrm((DEPTH, D_MODEL), 0.01),
    }


def reference(x, c, ctx, c_ctx, ada_w, ada_b, w_in, mlstm_conv_w, mlstm_conv_b, mlstm_gate_b,
              hyena_conv_w, hyena_conv_b, hyena_w1, hyena_b1, hyena_w2, hyena_b2, hyena_w3, hyena_freq, hyena_d,
              rwkv_mu, rwkv_w0, rwkv_w2, rwkv_a0, rwkv_a2, rwkv_g2, rwkv_kk, rwkv_ka, rwkv_rk,
              out_norm_g, out_norm_b, w_out, ln1_g, ln1_b, ffn_w_up, ffn_conv_w, ffn_conv_b, ffn_w_down,
              ln2_g, ln2_b):
    L = x.shape[1]
    ROWS = L // GRID_W
    freqs = ROPE_BASE ** (-jnp.arange(ROPE_PAIRS, dtype=F32) / ROPE_PAIRS)
    row = jnp.repeat(jnp.arange(ROWS, dtype=F32), GRID_W)
    col = jnp.tile(jnp.arange(GRID_W, dtype=F32), ROWS)
    ang = (row[:, None] * freqs, col[:, None] * freqs)
    xc = ctx
    silu_c = jax.nn.silu(c)
    silu_cc = jax.nn.silu(c_ctx)
    for l in range(DEPTH):
        last = l == DEPTH - 1
        mod = jnp.split((silu_c @ ada_w[l] + ada_b[l])[:, None, :], 6, axis=-1)
        modc = jnp.split(silu_cc @ ada_w[l] + ada_b[l], 6, axis=-1)
        p = {'w_in': w_in[l], 'mlstm_conv_w': mlstm_conv_w[l], 'mlstm_conv_b': mlstm_conv_b[l],
             'mlstm_gate_b': mlstm_gate_b[l], 'hyena_conv_w': hyena_conv_w[l], 'hyena_conv_b': hyena_conv_b[l],
             'hyena_w1': hyena_w1[l], 'hyena_b1': hyena_b1[l], 'hyena_w2': hyena_w2[l], 'hyena_b2': hyena_b2[l],
             'hyena_w3': hyena_w3[l], 'hyena_freq': hyena_freq[l], 'hyena_d': hyena_d[l],
             'rwkv_mu': rwkv_mu[l], 'rwkv_w0': rwkv_w0[l], 'rwkv_w2': rwkv_w2[l], 'rwkv_a0': rwkv_a0[l],
             'rwkv_a2': rwkv_a2[l], 'rwkv_g2': rwkv_g2[l], 'rwkv_kk': rwkv_kk[l], 'rwkv_ka': rwkv_ka[l],
             'rwkv_rk': rwkv_rk[l], 'out_norm_g': out_norm_g[l], 'out_norm_b': out_norm_b[l], 'w_out': w_out[l]}
        u = modulate(layer_norm(x), mod[0], mod[1])
        uc = modulate(layer_norm(xc), modc[0], modc[1])
        y, y_c = token_mixer(u, uc, p, ang, not last)
        x = layer_norm(DEEPNORM_ALPHA * x + mod[2] * y, ln1_g[l], ln1_b[l])
        u = modulate(layer_norm(x), mod[3], mod[4])
        f = conv_ffn(u, ffn_w_up[l], ffn_conv_w[l], ffn_conv_b[l], ffn_w_down[l])
        x = layer_norm(DEEPNORM_ALPHA * x + mod[5] * f, ln2_g[l], ln2_b[l])
        if not last:
            xc = layer_norm(DEEPNORM_ALPHA * xc + modc[2] * y_c, ln1_g[l], ln1_b[l])
            uc = modulate(layer_norm(xc), modc[3], modc[4])
            fc = conv_ffn(uc, ffn_w_up[l], ffn_conv_w[l], ffn_conv_b[l], ffn_w_down[l])
            xc = layer_norm(DEEPNORM_ALPHA * xc + modc[5] * fc, ln2_g[l], ln2_b[l])
    return x
```

```python
import functools
import math

import numpy as np
import jax
import jax.numpy as jnp
from jax import lax
from jax.experimental import pallas as pl
from jax.experimental.pallas import tpu as pltpu

F32 = jnp.float32
BF16 = jnp.bfloat16

HEAD_DIM = 64
GROUP_W = 256
N_HEADS_G = 4
CHUNK = 128
RWKV_CHUNK = 64
GRID_W = 64
ROPE_PAIRS = 16
ROPE_BASE = 10000.0
HYENA_EMB = 33
HYENA_BANDS = 16
HYENA_FAST_DECAY = 0.3
HYENA_SLOW_DECAY = 1.5
HYENA_TARGET = 1e-2
LN_EPS = 1e-5
GN_EPS = 1e-5
LANES = 128
SUBLANES = 8
VMEM_LIMIT = 56 * 1024 * 1024

NN = (((1,), (0,)), ((), ()))
NT = (((1,), (1,)), ((), ()))
TN = (((0,), (0,)), ((), ()))


def _dg(a, b, dims=NN):
    return lax.dot_general(a, b, dims, preferred_element_type=F32)


def _bf(x):
    return x.astype(BF16)


def _split2(x):
    hi = _bf(x)
    return hi, _bf(x - hi.astype(F32))


def _split3(x):
    hi = _bf(x)
    r = x - hi.astype(F32)
    mid = _bf(r)
    return hi, mid, _bf(r - mid.astype(F32))


def _mm(a, b, dims=NN, passes=1):
    if passes == 1:
        return _dg(_bf(a), _bf(b), dims)
    ah, al = _split2(a)
    bh, bl = _split2(b)
    return _dg(ah, bh, dims) + (_dg(al, bh, dims) + _dg(ah, bl, dims))


def _mm_xl(a_exact, b, dims=NN):
    b1, b2, b3 = _split3(b)
    return _dg(a_exact, b1, dims) + (_dg(a_exact, b2, dims) + _dg(a_exact, b3, dims))


def _mm_xr(a, b_exact, dims=NN):
    a1, a2 = _split2(a)
    return _dg(a1, b_exact, dims) + _dg(a2, b_exact, dims)


def _iota(shape, dim):
    return lax.broadcasted_iota(jnp.int32, shape, dim)


def _tri_mask(t, rev, strict=False):
    r = _iota((t, t), 0)
    c = _iota((t, t), 1)
    if strict:
        return (c > r) if rev else (c < r)
    return (c >= r) if rev else (c <= r)


def _head_ones():
    r = _iota((GROUP_W, GROUP_W), 0)
    c = _iota((GROUP_W, GROUP_W), 1)
    return jnp.where((r >> 6) == (c >> 6), 1.0, 0.0).astype(BF16)


def _head_sum(x, ones_bd):
    return _mm_xr(x, ones_bd)


def _head_norm(y, ones_bd, g, b):
    mu = _head_sum(y, ones_bd) * (1.0 / HEAD_DIM)
    yc = y - mu
    var = _head_sum(yc * yc, ones_bd) * (1.0 / HEAD_DIM)
    return yc * lax.rsqrt(var + GN_EPS) * g + b


def _ln_rows(x):
    mu = jnp.mean(x, -1, keepdims=True)
    xc = x - mu
    var = jnp.mean(xc * xc, -1, keepdims=True)
    return xc * lax.rsqrt(var + LN_EPS)


def _sigmoid(x):
    return 1.0 / (1.0 + jnp.exp(-x))


def _log_sigmoid(x):
    return jnp.minimum(x, 0.0) - jnp.log(1.0 + jnp.exp(-jnp.abs(x)))


def _with_neighbors(ref, s, t, ls, c0, c1):
    cur = ref[0, pl.ds(s, t), c0:c1]
    p0 = pl.multiple_of(jnp.maximum(s - SUBLANES, 0), SUBLANES)
    n0 = pl.multiple_of(jnp.minimum(s + t, ls - SUBLANES), SUBLANES)
    prow = ref[0, pl.ds(p0, SUBLANES), c0:c1][SUBLANES - 1:SUBLANES]
    nrow = ref[0, pl.ds(n0, SUBLANES), c0:c1][0:1]
    prow = jnp.where(s > 0, prow, 0.0)
    nrow = jnp.where(s + t < ls, nrow, 0.0)
    rows = _iota((t, 1), 0)
    prev = jnp.where(rows == 0, prow, pltpu.roll(cur, 1, 0))
    nxt = jnp.where(rows == t - 1, nrow, pltpu.roll(cur, t - 1, 0))
    return cur, prev, nxt


def _dwconv_rows(ref, s, t, ls, c0, c1, w_ref, b_ref):
    cur, prev, nxt = _with_neighbors(ref, s, t, ls, c0, c1)
    return prev * w_ref[0:1, c0:c1] + cur * w_ref[1:2, c0:c1] + nxt * w_ref[2:3, c0:c1] + b_ref[0:1, c0:c1]


def _cparams(sem):
    return pltpu.CompilerParams(dimension_semantics=sem, vmem_limit_bytes=VMEM_LIMIT)


def _const_spec(shape):
    nd = len(shape)
    return pl.BlockSpec(shape, lambda *_: (0,) * nd, pipeline_mode=pl.Buffered(1))


def _seq_spec(ls, w):
    return pl.BlockSpec((1, ls, w), lambda b: (b, 0, 0))


def _ada_kernel(c_ref, w_ref, b_ref, o_ref):
    c = c_ref[...]
    o_ref[...] = _mm(c * _sigmoid(c), w_ref[...], passes=3) + b_ref[...]


def _ada_call(cc, w, b):
    m, d = cc.shape
    n = w.shape[1]
    tn = 512
    return pl.pallas_call(
        _ada_kernel,
        grid=(n // tn,),
        in_specs=[pl.BlockSpec((m, d), lambda j: (0, 0)),
                  pl.BlockSpec((d, tn), lambda j: (0, j)),
                  pl.BlockSpec((1, tn), lambda j: (0, j))],
        out_specs=pl.BlockSpec((m, tn), lambda j: (0, j)),
        out_shape=jax.ShapeDtypeStruct((m, n), F32),
        compiler_params=_cparams(("arbitrary",)),
        name="ada_mod",
    )(cc, w, b.reshape(1, n))


SEG_W = (1152, 768, 1024, 896)


def _inproj_kernel(x_ref, sh_ref, sc_ref, w_ref, o1, o2, o3, o4):
    u = _ln_rows(x_ref[0]) * (1.0 + sc_ref[0]) + sh_ref[0]
    ub = _bf(u)
    off = 0
    for o in (o1, o2, o3, o4):
        n = o.shape[-1]
        o[0] = _dg(ub, w_ref[:, off:off + n])
        off += n


def _mod_spec(arr):
    d = arr.shape[-1]
    if arr.shape[0] == 1:
        return pl.BlockSpec((1, 1, d), lambda b, i: (0, 0, 0))
    return pl.BlockSpec((1, 1, d), lambda b, i: (b, 0, 0))


def _inproj_call(x, shift, scale, w_bf):
    bsz, ls, d = x.shape
    tm = min(512, ls)
    return pl.pallas_call(
        _inproj_kernel,
        grid=(bsz, ls // tm),
        in_specs=[pl.BlockSpec((1, tm, d), lambda b, i: (b, i, 0)),
                  _mod_spec(shift), _mod_spec(scale),
                  _const_spec(w_bf.shape)],
        out_specs=[pl.BlockSpec((1, tm, n), lambda b, i: (b, i, 0)) for n in SEG_W],
        out_shape=[jax.ShapeDtypeStruct((bsz, ls, n), F32) for n in SEG_W],
        compiler_params=_cparams(("parallel", "parallel")),
        name="in_proj",
    )(x, shift, scale, w_bf)


def _mlstm_chunk(q, k, v, gates, state, rev):
    t = q.shape[0]
    d = 1 if rev else 0
    tri = jnp.where(_tri_mask(t, rev), 1.0, 0.0).astype(BF16)
    mask = _tri_mask(t, rev)
    ls = _log_sigmoid(gates)
    bc = _mm_xl(tri, ls)
    gates_t = gates.T
    bc_t = bc.T
    last = 0 if rev else t - 1
    hs, new_state = [], []
    for h in range(N_HEADS_G):
        ci = 8 * d + h
        cf = 8 * d + 4 + h
        sl = slice(HEAD_DIM * h, HEAD_DIM * (h + 1))
        qh, kh, vh = q[:, sl], k[:, sl], v[:, sl]
        c_mem, n_mem, m_mem = state[h]
        ig_col = gates[:, ci:ci + 1]
        ig_row = gates_t[ci:ci + 1, :]
        b_col = bc[:, cf:cf + 1]
        b_row = bc_t[cf:cf + 1, :]
        b_last = bc[last:last + 1, cf:cf + 1]
        dlog = jnp.where(mask, b_col - b_row + ig_row, -jnp.inf)
        inter = b_col + m_mem
        m_t = jnp.maximum(inter, jnp.max(dlog, axis=-1, keepdims=True))
        s = _mm(qh, kh, NT) * jnp.exp(dlog - m_t)
        w_inter = jnp.exp(inter - m_t)
        num = _mm(s, vh) + w_inter * _mm(qh, c_mem)
        den = jnp.sum(s, -1, keepdims=True) + w_inter * jnp.sum(qh * n_mem, -1, keepdims=True)
        hs.append(num / jnp.maximum(jnp.abs(den), jnp.exp(-m_t)))
        g_log = b_last - b_col + ig_col
        m_new = jnp.maximum(b_last + m_mem, jnp.max(g_log, axis=0, keepdims=True))
        wk = jnp.exp(g_log - m_new)
        decay = jnp.exp(b_last + m_mem - m_new)
        kw = kh * wk
        c_new = decay * c_mem + _mm(kw, vh, TN)
        n_new = decay * n_mem + jnp.sum(kw, axis=0, keepdims=True)
        new_state.append((c_new, n_new, m_new))
    return jnp.concatenate(hs, axis=-1), tuple(new_state)


def _mlstm_kernel(p_ref, pc_ref, cw_ref, cb_ref, gb_ref, ng_ref, nb_ref, o_ref, oc_ref,
                  qk_ref, qkc_ref, hf_ref, hb_ref, hfc_ref, hbc_ref):
    t = CHUNK
    ones_bd = _head_ones()

    def conv_pass(src, dst):
        ls = src.shape[1]

        def body(c, carry):
            s = pl.multiple_of(c * t, t)
            y = _dwconv_rows(src, s, t, ls, 0, 2 * GROUP_W, cw_ref, cb_ref)
            dst[pl.ds(s, t), :] = y * _sigmoid(y)
            return carry

        lax.fori_loop(0, ls // t, body, 0)

    conv_pass(p_ref, qk_ref)
    conv_pass(pc_ref, qkc_ref)

    def scan(src, qk, hf, hb, states):
        ls = src.shape[1]
        n = ls // t

        def body(i, carry):
            out = []
            for rev, hbuf in ((False, hf), (True, hb)):
                c = (n - 1 - i) if rev else i
                s = pl.multiple_of(c * t, t)
                q = qk[pl.ds(s, t), 0:GROUP_W] * (HEAD_DIM ** -0.5)
                k = qk[pl.ds(s, t), GROUP_W:2 * GROUP_W]
                v = src[0, pl.ds(s, t), 2 * GROUP_W:3 * GROUP_W]
                gates = src[0, pl.ds(s, t), 4 * GROUP_W:4 * GROUP_W + LANES] + gb_ref[...]
                h, st = _mlstm_chunk(q, k, v, gates, carry[1 if rev else 0], rev)
                hbuf[pl.ds(s, t), :] = h
                out.append(st)
            return tuple(out)

        return lax.fori_loop(0, n, body, states)

    zero = tuple((jnp.zeros((HEAD_DIM, HEAD_DIM), F32), jnp.zeros((1, HEAD_DIM), F32), jnp.zeros((1, 1), F32))
                 for _ in range(N_HEADS_G))
    states = scan(pc_ref, qkc_ref, hfc_ref, hbc_ref, (zero, zero))
    scan(p_ref, qk_ref, hf_ref, hb_ref, states)

    def finish(src, hf, hb, out):
        ls = src.shape[1]

        def body(c, carry):
            s = pl.multiple_of(c * t, t)
            h = hf[pl.ds(s, t), :] + hb[pl.ds(s, t), :]
            o = src[0, pl.ds(s, t), 3 * GROUP_W:4 * GROUP_W]
            out[0, pl.ds(s, t), :] = _sigmoid(o) * _head_norm(h, ones_bd, ng_ref[...], nb_ref[...])
            return carry

        lax.fori_loop(0, ls // t, body, 0)

    finish(p_ref, hf_ref, hb_ref, o_ref)
    finish(pc_ref, hfc_ref, hbc_ref, oc_ref)


def _mixer_out(bsz, ls, lc):
    return ([_seq_spec(ls, GROUP_W), _seq_spec(lc, GROUP_W)],
            [jax.ShapeDtypeStruct((bsz, ls, GROUP_W), F32), jax.ShapeDtypeStruct((bsz, lc, GROUP_W), F32)])


def _mlstm_call(p, pc, conv_w, conv_b, gate_b, ng, nb):
    bsz, ls, w = p.shape
    lc = pc.shape[1]
    gb = jnp.zeros((1, LANES), F32).at[0, :4 * N_HEADS_G].set(gate_b.reshape(-1))
    consts = [conv_w, conv_b.reshape(1, -1), gb, ng.reshape(1, -1), nb.reshape(1, -1)]
    out_specs, out_shape = _mixer_out(bsz, ls, lc)
    return pl.pallas_call(
        _mlstm_kernel,
        grid=(bsz,),
        in_specs=[_seq_spec(ls, w), _seq_spec(lc, w)] + [_const_spec(a.shape) for a in consts],
        out_specs=out_specs,
        out_shape=out_shape,
        scratch_shapes=[pltpu.VMEM((ls, 2 * GROUP_W), F32), pltpu.VMEM((lc, 2 * GROUP_W), F32),
                        pltpu.VMEM((ls, GROUP_W), F32), pltpu.VMEM((ls, GROUP_W), F32),
                        pltpu.VMEM((lc, GROUP_W), F32), pltpu.VMEM((lc, GROUP_W), F32)],
        compiler_params=_cparams(("parallel",)),
        name="mlstm_mixer",
    )(p, pc, *consts)


def _ret_log_gamma(h, rev):
    hh = (N_HEADS_G - 1 - h) if rev else h
    return math.log(1.0 - 2.0 ** (-5.0 - hh))


def _ret_chunk(q, k, v, state, rev):
    t = q.shape[0]
    r = _iota((t, t), 0)
    c = _iota((t, t), 1)
    diff = ((c - r) if rev else (r - c)).astype(F32)
    valid = diff >= 0.0
    ordinal = _iota((t, 1), 0).astype(F32)
    if rev:
        ordinal = (t - 1.0) - ordinal
    hs, new_state = [], []
    for h in range(N_HEADS_G):
        lg = _ret_log_gamma(h, rev)
        sl = slice(HEAD_DIM * h, HEAD_DIM * (h + 1))
        qh, kh, vh = q[:, sl], k[:, sl], v[:, sl]
        decay_mask = jnp.where(valid, jnp.exp(jnp.where(valid, diff, 0.0) * lg), 0.0)
        q_dec = jnp.exp((ordinal + 1.0) * lg)
        k_dec = jnp.exp((t - 1.0 - ordinal) * lg)
        s = _mm(qh, kh, NT) * decay_mask
        hs.append(_mm(s, vh) + _mm(qh * q_dec, state[h]))
        new_state.append(state[h] * math.exp(t * lg) + _mm(kh * k_dec, vh, TN))
    return jnp.concatenate(hs, axis=-1), tuple(new_state)


def _rope(x, cos_t, sin_t):
    w = x.shape[-1]
    lane = _iota(x.shape, 1)
    partner = jnp.where((lane & 31) < ROPE_PAIRS, pltpu.roll(x, w - ROPE_PAIRS, 1), pltpu.roll(x, ROPE_PAIRS, 1))
    return x * cos_t + partner * sin_t


def _ret_kernel(p_ref, pc_ref, cos_ref, sin_ref, ng_ref, nb_ref, o_ref, oc_ref,
                hf_ref, hb_ref, hfc_ref, hbc_ref):
    t = CHUNK
    ones_bd = _head_ones()

    def scan(src, hf, hb, states, rotate):
        ls = src.shape[1]
        n = ls // t

        def body(i, carry):
            out = []
            for rev, hbuf in ((False, hf), (True, hb)):
                c = (n - 1 - i) if rev else i
                s = pl.multiple_of(c * t, t)
                q = src[0, pl.ds(s, t), 0:GROUP_W] * (HEAD_DIM ** -0.5)
                k = src[0, pl.ds(s, t), GROUP_W:2 * GROUP_W]
                v = src[0, pl.ds(s, t), 2 * GROUP_W:3 * GROUP_W]
                if rotate:
                    cos_t = cos_ref[pl.ds(s, t), :]
                    sin_t = sin_ref[pl.ds(s, t), :]
                    q = _rope(q, cos_t, sin_t)
                    k = _rope(k, cos_t, sin_t)
                h, st = _ret_chunk(q, k, v, carry[1 if rev else 0], rev)
                hbuf[pl.ds(s, t), :] = h
                out.append(st)
            return tuple(out)

        return lax.fori_loop(0, n, body, states)

    zero = tuple(jnp.zeros((HEAD_DIM, HEAD_DIM), F32) for _ in range(N_HEADS_G))
    states = scan(pc_ref, hfc_ref, hbc_ref, (zero, zero), False)
    scan(p_ref, hf_ref, hb_ref, states, True)

    def finish(src, hf, hb, out):
        ls = src.shape[1]

        def body(c, carry):
            s = pl.multiple_of(c * t, t)
            h = hf[pl.ds(s, t), :] + hb[pl.ds(s, t), :]
            g = src[0, pl.ds(s, t), 3 * GROUP_W:4 * GROUP_W]
            out[0, pl.ds(s, t), :] = g * _sigmoid(g) * _head_norm(h, ones_bd, ng_ref[...], nb_ref[...])
            return carry

        lax.fori_loop(0, ls // t, body, 0)

    finish(p_ref, hf_ref, hb_ref, o_ref)
    finish(pc_ref, hfc_ref, hbc_ref, oc_ref)


def _rope_tables(ls):
    pos = np.arange(ls)
    freqs = ROPE_BASE ** (-np.arange(ROPE_PAIRS, dtype=np.float64) / ROPE_PAIRS)
    row = (pos // GRID_W).astype(np.float64)[:, None] * freqs
    col = (pos % GRID_W).astype(np.float64)[:, None] * freqs
    ang = np.concatenate([row, row, col, col], -1)
    sign = np.tile(np.concatenate([-np.ones(ROPE_PAIRS), np.ones(ROPE_PAIRS)]), 2)
    cos_t = np.tile(np.cos(ang), (1, N_HEADS_G)).astype(np.float32)
    sin_t = np.tile(np.sin(ang) * sign, (1, N_HEADS_G)).astype(np.float32)
    return cos_t, sin_t


def _ret_call(p, pc, ng, nb):
    bsz, ls, w = p.shape
    lc = pc.shape[1]
    cos_t, sin_t = _rope_tables(ls)
    consts = [cos_t, sin_t, ng.reshape(1, -1), nb.reshape(1, -1)]
    out_specs, out_shape = _mixer_out(bsz, ls, lc)
    return pl.pallas_call(
        _ret_kernel,
        grid=(bsz,),
        in_specs=[_seq_spec(ls, w), _seq_spec(lc, w)] + [_const_spec(a.shape) for a in consts],
        out_specs=out_specs,
        out_shape=out_shape,
        scratch_shapes=[pltpu.VMEM((ls, GROUP_W), F32), pltpu.VMEM((ls, GROUP_W), F32),
                        pltpu.VMEM((lc, GROUP_W), F32), pltpu.VMEM((lc, GROUP_W), F32)],
        compiler_params=_cparams(("parallel",)),
        name="retention_mixer",
    )(p, pc, *consts)


RW_P = 3


def _rwkv_shifted(src, s, t, mu_ref):
    ls = src.shape[1]
    w = src.shape[2]
    cur, prev, nxt = _with_neighbors(src, s, t, ls, 0, w)
    return cur + mu_ref[0:1, :] * (prev - cur) + mu_ref[1:2, :] * (nxt - cur)


def _rwkv_chunk(z, prm, state, rev, ones_bd):
    w0_ref, w2_ref, a0_ref, a2_ref, kks_ref, ka_ref = prm
    t = z.shape[0]
    d = 1 if rev else 0
    g = GROUP_W
    r, k, v = z[:, 0:g], z[:, g:2 * g], z[:, 2 * g:3 * g]
    wlo = z[:, 3 * g:3 * g + 32]
    alo = z[:, 3 * g + 32:3 * g + 64]
    kk = k * kks_ref[...]
    kk = kk * lax.rsqrt(_head_sum(kk * kk, ones_bd) + 1e-12)
    w_pre = w0_ref[d:d + 1, :] + _mm(jnp.tanh(wlo), w2_ref[d], passes=RW_P)
    sp = jnp.maximum(-w_pre, 0.0) + jnp.log(1.0 + jnp.exp(-jnp.abs(w_pre)))
    logw = -jnp.exp(-sp - 0.5)
    ag = _sigmoid(a0_ref[d:d + 1, :] + _mm(alo, a2_ref[d], passes=RW_P))
    kd = k * (1.0 + (ag - 1.0) * ka_ref[...])
    a = -kk
    b = kk * ag
    tri = jnp.where(_tri_mask(t, rev), 1.0, 0.0).astype(BF16)
    cum = _mm_xl(tri, logw)
    e_pos = jnp.exp(cum)
    e_neg = jnp.exp(-cum)
    at = a * jnp.exp(cum - logw)
    rt = r * e_pos
    bt = b * e_neg
    kt = kd * e_neg
    strict = _tri_mask(t, rev, strict=True)
    incl = _tri_mask(t, rev)
    last = 0 if rev else t - 1
    n_lvl = int(math.log2(t))
    ys, new_state = [], []
    for h in range(N_HEADS_G):
        sl = slice(HEAD_DIM * h, HEAD_DIM * (h + 1))
        s0 = state[h]
        ar = jnp.concatenate([at[:, sl], rt[:, sl]], axis=0)
        bk = jnp.concatenate([bt[:, sl], kt[:, sl]], axis=0)
        vh = v[:, sl]
        gm = _mm(ar, bk, NT, passes=RW_P)
        hm = _mm(ar, s0, NT, passes=RW_P)
        l_ab = jnp.where(strict, gm[:t, :t], 0.0)
        l_ak = jnp.where(strict, gm[:t, t:], 0.0)
        m_rb = jnp.where(incl, gm[t:, :t], 0.0)
        m_rk = jnp.where(incl, gm[t:, t:], 0.0)
        u = hm[:t] + _mm(l_ak, vh, passes=RW_P)
        p = l_ab
        for lvl in range(n_lvl):
            u = u + _mm(p, u, passes=RW_P)
            if lvl + 1 < n_lvl:
                p = _mm(p, p, passes=RW_P)
        ys.append(hm[t:] + _mm(m_rb, u, passes=RW_P) + _mm(m_rk, vh, passes=RW_P))
        s_new = s0 + _mm(u, bt[:, sl], TN, passes=RW_P) + _mm(vh, kt[:, sl], TN, passes=RW_P)
        new_state.append(s_new * e_pos[last:last + 1, sl])
    return jnp.concatenate(ys, axis=-1), tuple(new_state)


def _rwkv_kernel(p_ref, pc_ref, mu_ref, w0_ref, w2_ref, a0_ref, a2_ref, g2_ref, kks_ref, ka_ref, rk_ref,
                 ng_ref, nb_ref, o_ref, oc_ref, hf_ref, hb_ref, hfc_ref, hbc_ref):
    t = RWKV_CHUNK
    ones_bd = _head_ones()
    prm = (w0_ref, w2_ref, a0_ref, a2_ref, kks_ref, ka_ref)

    def scan(src, hf, hb, states):
        ls = src.shape[1]
        n = ls // t

        def body(i, carry):
            out = []
            for rev, hbuf in ((False, hf), (True, hb)):
                c = (n - 1 - i) if rev else i
                s = pl.multiple_of(c * t, t)
                z = _rwkv_shifted(src, s, t, mu_ref)
                y, st = _rwkv_chunk(z, prm, carry[1 if rev else 0], rev, ones_bd)
                hbuf[pl.ds(s, t), :] = y
                out.append(st)
            return tuple(out)

        return lax.fori_loop(0, n, body, states)

    zero = tuple(jnp.zeros((HEAD_DIM, HEAD_DIM), F32) for _ in range(N_HEADS_G))
    states = scan(pc_ref, hfc_ref, hbc_ref, (zero, zero))
    scan(p_ref, hf_ref, hb_ref, states)

    def finish(src, hf, hb, out):
        ls = src.shape[1]
        g = GROUP_W

        def body(c, carry):
            s = pl.multiple_of(c * t, t)
            z = _rwkv_shifted(src, s, t, mu_ref)
            r, k, v = z[:, 0:g], z[:, g:2 * g], z[:, 2 * g:3 * g]
            glo = z[:, 3 * g + 64:3 * g + 128]
            gate = _mm(_sigmoid(glo), g2_ref[...])
            bonus = _head_sum(r * k * rk_ref[...], ones_bd) * v
            h = hf[pl.ds(s, t), :] + hb[pl.ds(s, t), :]
            out[0, pl.ds(s, t), :] = gate * (_head_norm(h, ones_bd, ng_ref[...], nb_ref[...]) + bonus)
            return carry

        lax.fori_loop(0, ls // t, body, 0)

    finish(p_ref, hf_ref, hb_ref, o_ref)
    finish(pc_ref, hfc_ref, hbc_ref, oc_ref)


def _rwkv_call(p, pc, mu, w0, w2, a0, a2, g2, kk, ka, rk, ng, nb):
    bsz, ls, w = p.shape
    lc = pc.shape[1]
    consts = [mu, w0, w2, a0, a2, g2, kk.reshape(1, -1), ka.reshape(1, -1), rk.reshape(1, -1),
              ng.reshape(1, -1), nb.reshape(1, -1)]
    out_specs, out_shape = _mixer_out(bsz, ls, lc)
    return pl.pallas_call(
        _rwkv_kernel,
        grid=(bsz,),
        in_specs=[_seq_spec(ls, w), _seq_spec(lc, w)] + [_const_spec(a.shape) for a in consts],
        out_specs=out_specs,
        out_shape=out_shape,
        scratch_shapes=[pltpu.VMEM((ls, GROUP_W), F32), pltpu.VMEM((ls, GROUP_W), F32),
                        pltpu.VMEM((lc, GROUP_W), F32), pltpu.VMEM((lc, GROUP_W), F32)],
        compiler_params=_cparams(("parallel",)),
        name="rwkv7_mixer",
    )(p, pc, *consts)


def _dft_mats(ls):
    n = 2 * ls
    idx = np.arange(ls, dtype=np.int64)
    prod = (idx[:, None] * idx[None, :]) % n
    ang = 2.0 * np.pi * prod.astype(np.float64) / n
    return np.cos(ang), -np.sin(ang)


def _hyena_feats(ls):
    pos = np.arange(ls, dtype=np.float64)
    tt = np.linspace(0.0, 1.0, ls)[:, None]
    ang = 2.0 * math.pi * pos[:, None] / ls
    bands = np.linspace(1e-4, HYENA_BANDS - 1, HYENA_BANDS)[None, :]
    feats = np.concatenate([tt, np.cos(bands * ang), -np.sin(bands * ang)], -1)
    feats = np.pad(feats, ((0, 0), (0, LANES - HYENA_EMB))).astype(np.float32)
    max_decay = math.log(HYENA_TARGET) / HYENA_FAST_DECAY
    min_decay = math.log(HYENA_TARGET) / HYENA_SLOW_DECAY
    deltas = np.abs(np.linspace(min_decay, max_decay, GROUP_W))
    window = np.exp(-tt * deltas).astype(np.float32)
    return feats, window


def _hyena_taps_kernel(f_ref, win_ref, w1_ref, b1_ref, w2_ref, b2_ref, w3_ref, fr_ref, sum_ref, dif_ref, nyq_ref):
    fr = fr_ref[...]
    h = jnp.sin(fr * (_mm(f_ref[...], w1_ref[...], passes=3) + b1_ref[...]))
    h = jnp.sin(fr * (_mm(h, w2_ref[...], passes=3) + b2_ref[...]))
    h = _mm(h, w3_ref[...], passes=3)
    win = win_ref[...]
    win2 = jnp.concatenate([win, win], axis=-1)
    hf = h[:, 0:2 * GROUP_W] * win2
    hb = h[:, 2 * GROUP_W:4 * GROUP_W] * win2
    sum_ref[...] = hf + hb
    dif_ref[...] = hf - hb
    ls = h.shape[0]
    alt = 1.0 - 2.0 * (_iota((ls, 1), 0) & 1).astype(F32)
    nyq_ref[...] = jnp.sum((hf + hb) * alt, axis=0, keepdims=True) * (1.0 / (2.0 * ls))


def _hyena_spec_kernel(fch_ref, fcl_ref, fsh_ref, fsl_ref, sum_ref, dif_ref, kre_ref, kim_ref):
    j = pl.program_id(0)
    tf = fch_ref.shape[0]
    ls = fch_ref.shape[1]
    hs_h, hs_l = _split2(sum_ref[...])
    hd_h, hd_l = _split2(dif_ref[...])
    kre = _dg(fch_ref[...], hs_h) + (_dg(fcl_ref[...], hs_h) + _dg(fch_ref[...], hs_l))
    kim = _dg(fsh_ref[...], hd_h) + (_dg(fsl_ref[...], hd_h) + _dg(fsh_ref[...], hd_l))
    f_idx = _iota((tf, 1), 0) + j * tf
    scale = jnp.where(f_idx == 0, 1.0, 2.0) * (1.0 / (2.0 * ls))
    kre_ref[...] = kre * scale
    kim_ref[...] = kim * scale


def _hyena_filters(ls, w1, b1, w2, b2, w3, freq, fmats):
    feats, window = _hyena_feats(ls)
    w1p = jnp.pad(w1, ((0, LANES - HYENA_EMB), (0, 0)))
    args = [jnp.asarray(feats), jnp.asarray(window), w1p, b1.reshape(1, -1), w2, b2.reshape(1, -1), w3,
            freq.reshape(1, -1)]
    hsum, hdif, knyq = pl.pallas_call(
        _hyena_taps_kernel,
        out_shape=[jax.ShapeDtypeStruct((ls, 2 * GROUP_W), F32)] * 2 + [jax.ShapeDtypeStruct((1, 2 * GROUP_W), F32)],
        compiler_params=pltpu.CompilerParams(vmem_limit_bytes=VMEM_LIMIT),
        name="hyena_taps",
    )(*args)
    fch, fcl, fsh, fsl = fmats
    tf = min(512, ls)
    fspec = pl.BlockSpec((tf, ls), lambda j: (j, 0))
    tspec = pl.BlockSpec((ls, 2 * GROUP_W), lambda j: (0, 0))
    ospec = pl.BlockSpec((tf, 2 * GROUP_W), lambda j: (j, 0))
    kre, kim = pl.pallas_call(
        _hyena_spec_kernel,
        grid=(ls // tf,),
        in_specs=[fspec, fspec, fspec, fspec, tspec, tspec],
        out_specs=[ospec, ospec],
        out_shape=[jax.ShapeDtypeStruct((ls, 2 * GROUP_W), F32)] * 2,
        compiler_params=_cparams(("arbitrary",)),
        name="hyena_spectra",
    )(fch, fcl, fsh, fsl, hsum, hdif)
    return kre, kim, knyq


def _hyena_kernel(p_ref, fc_ref, fs_ref, kre_ref, kim_ref, knyq_ref, cw_ref, cb_ref, d_ref, ng_ref, nb_ref,
                  o_ref, ub_ref, yre_ref, yim_ref, z_ref, nyq_ref):
    ls = p_ref.shape[1]
    tf = min(512, ls)
    nt = ls // tf
    g = GROUP_W
    ones_bd = _head_ones()

    def alt_sign(s):
        rows = _iota((tf, 1), 0) + s
        return 1.0 - 2.0 * (rows & 1).astype(F32)

    def long_conv(order):
        c0 = order * g

        def fwd(j, carry):
            s = pl.multiple_of(j * tf, tf)
            ure = _dg(fc_ref[pl.ds(s, tf), :], ub_ref[...])
            uim = _dg(fs_ref[pl.ds(s, tf), :], ub_ref[...])
            kre = kre_ref[pl.ds(s, tf), c0:c0 + g]
            kim = kim_ref[pl.ds(s, tf), c0:c0 + g]
            yre_ref[pl.ds(s, tf), :] = _bf(ure * kre - uim * kim)
            yim_ref[pl.ds(s, tf), :] = _bf(ure * kim + uim * kre)
            return carry

        lax.fori_loop(0, nt, fwd, 0)

    def inv_tile(s, order):
        c0 = order * g
        y = _dg(fc_ref[pl.ds(s, tf), :], yre_ref[...]) + _dg(fs_ref[pl.ds(s, tf), :], yim_ref[...])
        return y + alt_sign(s) * (nyq_ref[...] * knyq_ref[0:1, c0:c0 + g])

    def prep(j, acc):
        s = pl.multiple_of(j * tf, tf)
        v = _dwconv_rows(p_ref, s, tf, ls, 0, g, cw_ref, cb_ref)
        z_ref[pl.ds(s, tf), :] = v
        ub_ref[pl.ds(s, tf), :] = _bf(v)
        return acc + jnp.sum(v * alt_sign(s), axis=0, keepdims=True)

    nyq_ref[...] = lax.fori_loop(0, nt, prep, jnp.zeros((1, g), F32))
    long_conv(0)

    def mid(j, acc):
        s = pl.multiple_of(j * tf, tf)
        v = z_ref[pl.ds(s, tf), :]
        x1 = _dwconv_rows(p_ref, s, tf, ls, g, 2 * g, cw_ref, cb_ref)
        z2 = x1 * (inv_tile(s, 0) + v * d_ref[0:1, :])
        z_ref[pl.ds(s, tf), :] = z2
        ub_ref[pl.ds(s, tf), :] = _bf(z2)
        return acc + jnp.sum(z2 * alt_sign(s), axis=0, keepdims=True)

    nyq2 = lax.fori_loop(0, nt, mid, jnp.zeros((1, g), F32))
    nyq_ref[...] = nyq2
    long_conv(1)

    def fin(j, carry):
        s = pl.multiple_of(j * tf, tf)
        z2 = z_ref[pl.ds(s, tf), :]
        x2 = _dwconv_rows(p_ref, s, tf, ls, 2 * g, 3 * g, cw_ref, cb_ref)
        y = x2 * (inv_tile(s, 1) + z2 * d_ref[1:2, :])
        o_ref[0, pl.ds(s, tf), :] = _head_norm(y, ones_bd, ng_ref[...], nb_ref[...])
        return carry

    lax.fori_loop(0, nt, fin, 0)


def _hyena_call(p, fc, fs, kre, kim, knyq, conv_w, conv_b, d_bias, ng, nb):
    bsz, ls, w = p.shape
    consts = [fc, fs, kre, kim, knyq, conv_w, conv_b.reshape(1, -1), d_bias, ng.reshape(1, -1), nb.reshape(1, -1)]
    return pl.pallas_call(
        _hyena_kernel,
        grid=(bsz,),
        in_specs=[_seq_spec(ls, w)] + [_const_spec(a.shape) for a in consts],
        out_specs=_seq_spec(ls, GROUP_W),
        out_shape=jax.ShapeDtypeStruct((bsz, ls, GROUP_W), F32),
        scratch_shapes=[pltpu.VMEM((ls, GROUP_W), BF16), pltpu.VMEM((ls, GROUP_W), BF16),
                        pltpu.VMEM((ls, GROUP_W), BF16), pltpu.VMEM((ls, GROUP_W), F32),
                        pltpu.VMEM((1, GROUP_W), F32)],
        compiler_params=_cparams(("parallel",)),
        name="hyena_mixer",
    )(p, *consts)


def _outproj_kernel(ya, yb, yr, yw, x_ref, gate_ref, w_ref, g_ref, b_ref, o_ref, *, alpha):
    y = None
    for i, part in enumerate((ya, yb, yr, yw)):
        c = _dg(_bf(part[0]), w_ref[i * GROUP_W:(i + 1) * GROUP_W, :])
        y = c if y is None else y + c
    xn = alpha * x_ref[0] + gate_ref[0] * y
    o_ref[0] = _ln_rows(xn) * g_ref[...] + b_ref[...]


def _outproj_call(parts, x, gate, w_bf, g, b, alpha):
    bsz, ls, d = x.shape
    tm = min(512, ls)
    row = lambda w: pl.BlockSpec((1, tm, w), lambda bb, i: (bb, i, 0))
    consts = [w_bf, g.reshape(1, -1), b.reshape(1, -1)]
    return pl.pallas_call(
        functools.partial(_outproj_kernel, alpha=alpha),
        grid=(bsz, ls // tm),
        in_specs=[row(GROUP_W)] * 4 + [row(d), _mod_spec(gate)] + [_const_spec(a.shape) for a in consts],
        out_specs=row(d),
        out_shape=jax.ShapeDtypeStruct((bsz, ls, d), F32),
        compiler_params=_cparams(("parallel", "parallel")),
        name="out_proj_ln",
    )(*parts, x, gate, *consts)


def _ffn_kernel(xp_ref, x_ref, xn_ref, sh_ref, sc_ref, gate_ref, wa_ref, wb_ref, cw_ref, cb_ref, wd_ref,
                g_ref, b_ref, o_ref, *, alpha):
    i = pl.program_id(1)
    nt = pl.num_programs(1)
    tl = x_ref.shape[1]
    h8 = SUBLANES
    xm = x_ref[0]
    xa = jnp.concatenate([xp_ref[0], xm, xn_ref[0]], axis=0)
    u = _bf(_ln_rows(xa) * (1.0 + sc_ref[0]) + sh_ref[0])
    rows = _iota((tl + 2 * h8, 1), 0)
    valid = jnp.logical_and(jnp.logical_or(rows >= h8, i > 0), jnp.logical_or(rows < tl + h8, i < nt - 1))
    um = u[h8:tl + h8]
    nj = wa_ref.shape[0]

    def body(j, acc):
        a = jnp.where(valid, _dg(u, wa_ref[j]), 0.0)
        bb = _dg(um, wb_ref[j])
        cw = cw_ref[j]
        prev = pltpu.roll(a, 1, 0)[h8:tl + h8]
        nxt = pltpu.roll(a, tl + 2 * h8 - 1, 0)[h8:tl + h8]
        ac = prev * cw[0:1] + a[h8:tl + h8] * cw[1:2] + nxt * cw[2:3] + cb_ref[j]
        hid = ac * _sigmoid(ac) * bb
        return acc + _dg(_bf(hid), wd_ref[j])

    f = lax.fori_loop(0, nj, body, jnp.zeros(xm.shape, F32))
    xn = alpha * xm + gate_ref[0] * f
    o_ref[0] = _ln_rows(xn) * g_ref[...] + b_ref[...]


def _ffn_call(x, shift, scale, gate, wa, wb, cw, cb, wd, g, b, alpha):
    bsz, ls, d = x.shape
    tl = min(512, ls)
    nb8 = tl // SUBLANES
    last8 = ls // SUBLANES - 1
    consts = [wa, wb, cw, cb, wd, g.reshape(1, -1), b.reshape(1, -1)]
    return pl.pallas_call(
        functools.partial(_ffn_kernel, alpha=alpha),
        grid=(bsz, ls // tl),
        in_specs=[pl.BlockSpec((1, SUBLANES, d), lambda bb, i: (bb, jnp.maximum(i * nb8 - 1, 0), 0)),
                  pl.BlockSpec((1, tl, d), lambda bb, i: (bb, i, 0)),
                  pl.BlockSpec((1, SUBLANES, d), lambda bb, i: (bb, jnp.minimum((i + 1) * nb8, last8), 0)),
                  _mod_spec(shift), _mod_spec(scale), _mod_spec(gate)] + [_const_spec(a.shape) for a in consts],
        out_specs=pl.BlockSpec((1, tl, d), lambda bb, i: (bb, i, 0)),
        out_shape=jax.ShapeDtypeStruct((bsz, ls, d), F32),
        compiler_params=_cparams(("parallel", "parallel")),
        name="conv_ffn_ln",
    )(x, x, x, shift, scale, gate, *consts)


FFN_COLS = 256


def _prep_w_in(w):
    d = w.shape[0]
    m = 4 * GROUP_W + 4 * N_HEADS_G
    pad = jnp.zeros((d, SEG_W[0] - m), w.dtype)
    return _bf(jnp.concatenate([w[:, :m], pad, w[:, m:]], axis=1))


def _prep_ffn(w_up, conv_w, conv_b, w_down):
    d, two_ff = w_up.shape
    dff = two_ff // 2
    nj = dff // FFN_COLS
    wa = _bf(w_up[:, :dff]).reshape(d, nj, FFN_COLS).transpose(1, 0, 2)
    wb = _bf(w_up[:, dff:]).reshape(d, nj, FFN_COLS).transpose(1, 0, 2)
    cw = conv_w.reshape(3, nj, FFN_COLS).transpose(1, 0, 2)
    cb = conv_b.reshape(nj, 1, FFN_COLS)
    wd = _bf(w_down).reshape(nj, FFN_COLS, d)
    return wa, wb, cw, cb, wd


@functools.lru_cache(maxsize=None)
def _dft_inputs(ls):
    out = []
    for m in _dft_mats(ls):
        hi = m.astype(np.float32).astype(BF16)
        lo = (m - hi.astype(np.float64)).astype(np.float32).astype(BF16)
        out += [hi, lo]
    return tuple(out)


def kernel(x, c, ctx, c_ctx, ada_w, ada_b, w_in, mlstm_conv_w, mlstm_conv_b, mlstm_gate_b, hyena_conv_w, hyena_conv_b, hyena_w1, hyena_b1, hyena_w2, hyena_b2, hyena_w3, hyena_freq, hyena_d, rwkv_mu, rwkv_w0, rwkv_w2, rwkv_a0, rwkv_a2, rwkv_g2, rwkv_kk, rwkv_ka, rwkv_rk, out_norm_g, out_norm_b, w_out, ln1_g, ln1_b, ffn_w_up, ffn_conv_w, ffn_conv_b, ffn_w_down, ln2_g, ln2_b):
    bsz, ls, d = x.shape
    lc = ctx.shape[1]
    depth = ada_w.shape[0]
    alpha = float((2 * depth) ** 0.25)
    g = GROUP_W

    n_rows = -(-(bsz + 1) // SUBLANES) * SUBLANES
    cc = jnp.concatenate([c, c_ctx[None, :], jnp.zeros((n_rows - bsz - 1, d), F32)], axis=0)

    dft_l = _dft_inputs(ls)
    dft_c = _dft_inputs(lc)

    xc = ctx
    for l in range(depth):
        last = l == depth - 1
        mod_all = _ada_call(cc, ada_w[l], ada_b[l])
        mod = [mod_all[:bsz, i * d:(i + 1) * d].reshape(bsz, 1, d) for i in range(6)]
        modc = [mod_all[bsz:bsz + 1, i * d:(i + 1) * d].reshape(1, 1, d) for i in range(6)]
        w_in_bf = _prep_w_in(w_in[l])
        ng, nb = out_norm_g[l], out_norm_b[l]
        gs = lambda i, t: t[i * g:(i + 1) * g]

        pa, ph, pr, pw = _inproj_call(x, mod[0], mod[1], w_in_bf)
        pa_c, ph_c, pr_c, pw_c = _inproj_call(xc, modc[0], modc[1], w_in_bf)

        ya, ya_c = _mlstm_call(pa, pa_c, mlstm_conv_w[l], mlstm_conv_b[l], mlstm_gate_b[l], gs(0, ng), gs(0, nb))
        hy = (hyena_w1[l], hyena_b1[l], hyena_w2[l], hyena_b2[l], hyena_w3[l], hyena_freq[l])
        kre, kim, knyq = _hyena_filters(ls, *hy, dft_l)
        yb = _hyena_call(ph, dft_l[0], dft_l[2], kre, kim, knyq, hyena_conv_w[l], hyena_conv_b[l], hyena_d[l],
                         gs(1, ng), gs(1, nb))
        yr, yr_c = _ret_call(pr, pr_c, gs(2, ng), gs(2, nb))
        yw, yw_c = _rwkv_call(pw, pw_c, rwkv_mu[l], rwkv_w0[l], rwkv_w2[l], rwkv_a0[l], rwkv_a2[l], rwkv_g2[l],
                              rwkv_kk[l], rwkv_ka[l], rwkv_rk[l], gs(3, ng), gs(3, nb))

        w_out_bf = _bf(w_out[l])
        ffn_w = _prep_ffn(ffn_w_up[l], ffn_conv_w[l], ffn_conv_b[l], ffn_w_down[l])
        x = _outproj_call((ya, yb, yr, yw), x, mod[2], w_out_bf, ln1_g[l], ln1_b[l], alpha)
        x = _ffn_call(x, mod[3], mod[4], mod[5], *ffn_w, ln2_g[l], ln2_b[l], alpha)
        if not last:
            kre_c, kim_c, knyq_c = _hyena_filters(lc, *hy, dft_c)
            yb_c = _hyena_call(ph_c, dft_c[0], dft_c[2], kre_c, kim_c, knyq_c, hyena_conv_w[l], hyena_conv_b[l],
                               hyena_d[l], gs(1, ng), gs(1, nb))
            xc = _outproj_call((ya_c, yb_c, yr_c, yw_c), xc, modc[2], w_out_bf, ln1_g[l], ln1_b[l], alpha)
            xc = _ffn_call(xc, modc[3], modc[4], modc[5], *ffn_w, ln2_g[l], ln2_b[l], alpha)
    return x
```

```python
import functools
import math

import numpy as np
import jax
import jax.numpy as jnp
from jax import lax
from jax.experimental import pallas as pl
from jax.experimental.pallas import tpu as pltpu

F32 = jnp.float32
BF16 = jnp.bfloat16

HEAD_DIM = 64
GROUP_W = 256
N_HEADS_G = 4
CHUNK = 128
RWKV_CHUNK = 64
GRID_W = 64
ROPE_PAIRS = 16
ROPE_BASE = 10000.0
HYENA_EMB = 33
HYENA_BANDS = 16
HYENA_FAST_DECAY = 0.3
HYENA_SLOW_DECAY = 1.5
HYENA_TARGET = 1e-2
LN_EPS = 1e-5
GN_EPS = 1e-5
LANES = 128
SUBLANES = 8
VMEM_LIMIT = 56 * 1024 * 1024

NN = (((1,), (0,)), ((), ()))
NT = (((1,), (1,)), ((), ()))
TN = (((0,), (0,)), ((), ()))


def _dg(a, b, dims=NN):
    return lax.dot_general(a, b, dims, preferred_element_type=F32)


def _bf(x):
    return x.astype(BF16)


def _split2(x):
    hi = _bf(x)
    return hi, _bf(x - hi.astype(F32))


def _split3(x):
    hi = _bf(x)
    r = x - hi.astype(F32)
    mid = _bf(r)
    return hi, mid, _bf(r - mid.astype(F32))


def _mm(a, b, dims=NN, passes=1):
    if passes == 1:
        return _dg(_bf(a), _bf(b), dims)
    ah, al = _split2(a)
    bh, bl = _split2(b)
    return _dg(ah, bh, dims) + (_dg(al, bh, dims) + _dg(ah, bl, dims))


def _mm_xl(a_exact, b, dims=NN):
    b1, b2, b3 = _split3(b)
    return _dg(a_exact, b1, dims) + (_dg(a_exact, b2, dims) + _dg(a_exact, b3, dims))


def _mm_xr(a, b_exact, dims=NN):
    a1, a2 = _split2(a)
    return _dg(a1, b_exact, dims) + _dg(a2, b_exact, dims)


def _iota(shape, dim):
    return lax.broadcasted_iota(jnp.int32, shape, dim)


def _tri_mask(t, rev, strict=False):
    r = _iota((t, t), 0)
    c = _iota((t, t), 1)
    if strict:
        return (c > r) if rev else (c < r)
    return (c >= r) if rev else (c <= r)


def _head_ones():
    r = _iota((GROUP_W, GROUP_W), 0)
    c = _iota((GROUP_W, GROUP_W), 1)
    return jnp.where((r >> 6) == (c >> 6), 1.0, 0.0).astype(BF16)


def _head_sum(x, ones_bd):
    return _mm_xr(x, ones_bd)


def _head_norm(y, ones_bd, g, b):
    mu = _head_sum(y, ones_bd) * (1.0 / HEAD_DIM)
    yc = y - mu
    var = _head_sum(yc * yc, ones_bd) * (1.0 / HEAD_DIM)
    return yc * lax.rsqrt(var + GN_EPS) * g + b


def _ln_rows(x):
    mu = jnp.mean(x, -1, keepdims=True)
    xc = x - mu
    var = jnp.mean(xc * xc, -1, keepdims=True)
    return xc * lax.rsqrt(var + LN_EPS)


def _sigmoid(x):
    return 1.0 / (1.0 + jnp.exp(-x))


def _log_sigmoid(x):
    return jnp.minimum(x, 0.0) - jnp.log(1.0 + jnp.exp(-jnp.abs(x)))


def _with_neighbors(ref, s, t, ls, c0, c1):
    cur = ref[0, pl.ds(s, t), c0:c1]
    p0 = pl.multiple_of(jnp.maximum(s - SUBLANES, 0), SUBLANES)
    n0 = pl.multiple_of(jnp.minimum(s + t, ls - SUBLANES), SUBLANES)
    prow = ref[0, pl.ds(p0, SUBLANES), c0:c1][SUBLANES - 1:SUBLANES]
    nrow = ref[0, pl.ds(n0, SUBLANES), c0:c1][0:1]
    prow = jnp.where(s > 0, prow, 0.0)
    nrow = jnp.where(s + t < ls, nrow, 0.0)
    rows = _iota((t, 1), 0)
    prev = jnp.where(rows == 0, prow, pltpu.roll(cur, 1, 0))
    nxt = jnp.where(rows == t - 1, nrow, pltpu.roll(cur, t - 1, 0))
    return cur, prev, nxt


def _dwconv_rows(ref, s, t, ls, c0, c1, w_ref, b_ref):
    cur, prev, nxt = _with_neighbors(ref, s, t, ls, c0, c1)
    return prev * w_ref[0:1, c0:c1] + cur * w_ref[1:2, c0:c1] + nxt * w_ref[2:3, c0:c1] + b_ref[0:1, c0:c1]


def _cparams(sem):
    return pltpu.CompilerParams(dimension_semantics=sem, vmem_limit_bytes=VMEM_LIMIT)


def _const_spec(shape):
    nd = len(shape)
    return pl.BlockSpec(shape, lambda *_: (0,) * nd, pipeline_mode=pl.Buffered(1))


def _seq_spec(ls, w):
    return pl.BlockSpec((1, ls, w), lambda b: (b, 0, 0))


def _ada_kernel(c_ref, w_ref, b_ref, o_ref):
    c = c_ref[...]
    o_ref[...] = _mm(c * _sigmoid(c), w_ref[...], passes=3) + b_ref[...]


def _ada_call(cc, w, b):
    m, d = cc.shape
    n = w.shape[1]
    tn = 512
    return pl.pallas_call(
        _ada_kernel,
        grid=(n // tn,),
        in_specs=[pl.BlockSpec((m, d), lambda j: (0, 0)),
                  pl.BlockSpec((d, tn), lambda j: (0, j)),
                  pl.BlockSpec((1, tn), lambda j: (0, j))],
        out_specs=pl.BlockSpec((m, tn), lambda j: (0, j)),
        out_shape=jax.ShapeDtypeStruct((m, n), F32),
        compiler_params=_cparams(("arbitrary",)),
        name="ada_mod",
    )(cc, w, b.reshape(1, n))


SEG_W = (1152, 768, 1024, 896)


def _inproj_kernel(x_ref, sh_ref, sc_ref, w_ref, o1, o2, o3, o4):
    u = _ln_rows(x_ref[0]) * (1.0 + sc_ref[0]) + sh_ref[0]
    ub = _bf(u)
    off = 0
    for o in (o1, o2, o3, o4):
        n = o.shape[-1]
        o[0] = _dg(ub, w_ref[:, off:off + n])
        off += n


def _mod_spec(arr):
    d = arr.shape[-1]
    if arr.shape[0] == 1:
        return pl.BlockSpec((1, 1, d), lambda b, i: (0, 0, 0))
    return pl.BlockSpec((1, 1, d), lambda b, i: (b, 0, 0))


def _inproj_call(x, shift, scale, w_bf):
    bsz, ls, d = x.shape
    tm = min(512, ls)
    return pl.pallas_call(
        _inproj_kernel,
        grid=(bsz, ls // tm),
        in_specs=[pl.BlockSpec((1, tm, d), lambda b, i: (b, i, 0)),
                  _mod_spec(shift), _mod_spec(scale),
                  _const_spec(w_bf.shape)],
        out_specs=[pl.BlockSpec((1, tm, n), lambda b, i: (b, i, 0)) for n in SEG_W],
        out_shape=[jax.ShapeDtypeStruct((bsz, ls, n), F32) for n in SEG_W],
        compiler_params=_cparams(("parallel", "parallel")),
        name="in_proj",
    )(x, shift, scale, w_bf)


def _mlstm_solve(chunks, states):
    t = chunks[0][0].shape[0]
    hd = HEAD_DIM
    lane = _iota((t, LANES), 1)
    lo = lane < hd
    chains = []
    for di, (q, k, v, gates, rev) in enumerate(chunks):
        d = 1 if rev else 0
        mask = _tri_mask(t, rev)
        tri = jnp.where(mask, 1.0, 0.0).astype(BF16)
        bc = _mm_xl(tri, _log_sigmoid(gates))
        gates_t = gates.T
        bc_t = bc.T
        last = 0 if rev else t - 1
        for h in range(N_HEADS_G):
            p, e = divmod(h, 2)
            ci, cf = 8 * d + h, 8 * d + 4 + h
            cols = slice(LANES * p, LANES * (p + 1))
            half = lo if e == 0 else jnp.logical_not(lo)
            den_lane = hd if e == 0 else 0
            chains.append(dict(
                di=di, h=h, p=p, e=e, half=half, den_lane=den_lane, mask=mask,
                q=q[:, cols], k=k[:, cols], v=v[:, cols],
                ig_col=gates[:, ci:ci + 1], ig_row=gates_t[ci:ci + 1, :],
                b_col=bc[:, cf:cf + 1], b_row=bc_t[cf:cf + 1, :], b_last=bc[last:last + 1, cf:cf + 1],
                cn=states[di][0][p], m=states[di][1][h]))
    for c in chains:
        c["qm"] = jnp.where(c["half"], c["q"], 0.0)
        c["s_raw"] = _dg(_bf(c["qm"]), _bf(c["k"]), NT)
    for c in chains:
        dlog = jnp.where(c["mask"], c["b_col"] - c["b_row"] + c["ig_row"], -jnp.inf)
        inter = c["b_col"] + c["m"]
        c["m_t"] = jnp.maximum(inter, jnp.max(dlog, axis=-1, keepdims=True))
        c["s"] = c["s_raw"] * jnp.exp(dlog - c["m_t"])
        c["w_inter"] = jnp.exp(inter - c["m_t"])
    for c in chains:
        c["v_aug"] = _bf(jnp.where(c["half"], c["v"], jnp.where(lane == c["den_lane"], 1.0, 0.0)))
        lhs = jnp.concatenate([_bf(c["s"]), _bf(c["qm"] * c["w_inter"])], axis=1)
        rhs = jnp.concatenate([c["v_aug"], _bf(c["cn"])], axis=0)
        c["res"] = _dg(lhs, rhs)
    for c in chains:
        dl = c["den_lane"]
        den = c["res"][:, dl:dl + 1]
        c["hh"] = c["res"] / jnp.maximum(jnp.abs(den), jnp.exp(-c["m_t"]))
    for c in chains:
        g_log = c["b_last"] - c["b_col"] + c["ig_col"]
        c["m_new"] = jnp.maximum(c["b_last"] + c["m"], jnp.max(g_log, axis=0, keepdims=True))
        c["kw"] = _bf(jnp.where(c["half"], c["k"], 0.0) * jnp.exp(g_log - c["m_new"]))
        c["decay"] = jnp.exp(c["b_last"] + c["m"] - c["m_new"])
    out = []
    row = _iota((LANES, 1), 0)
    for di in range(len(chunks)):
        hs, cns = [], []
        for p in range(N_HEADS_G // 2):
            ce, co = [c for c in chains if c["di"] == di and c["p"] == p]
            hs.append(jnp.where(lo, ce["hh"], co["hh"]))
            upd = _dg(jnp.concatenate([ce["kw"], co["kw"]], axis=0),
                      jnp.concatenate([ce["v_aug"], co["v_aug"]], axis=0), TN)
            cns.append(jnp.where(row < hd, ce["decay"], co["decay"]) * ce["cn"] + upd)
        ms = tuple(c["m_new"] for c in chains if c["di"] == di)
        out.append((jnp.concatenate(hs, axis=-1), (tuple(cns), ms)))
    return out


def _mlstm_kernel(p_ref, pc_ref, cw_ref, cb_ref, gb_ref, ng_ref, nb_ref, o_ref, oc_ref,
                  qk_ref, qkc_ref, hf_ref, hb_ref, hfc_ref, hbc_ref):
    t = CHUNK
    ones_bd = _head_ones()

    def conv_pass(src, dst):
        ls = src.shape[1]

        def body(c, carry):
            s = pl.multiple_of(c * t, t)
            y = _dwconv_rows(src, s, t, ls, 0, 2 * GROUP_W, cw_ref, cb_ref)
            dst[pl.ds(s, t), :] = y * _sigmoid(y)
            return carry

        lax.fori_loop(0, ls // t, body, 0)

    conv_pass(p_ref, qk_ref)
    conv_pass(pc_ref, qkc_ref)

    def scan(src, qk, hf, hb, states):
        ls = src.shape[1]
        n = ls // t

        def body(i, carry):
            starts = (pl.multiple_of(i * t, t), pl.multiple_of((n - 1 - i) * t, t))
            chunks = []
            for s, rev in zip(starts, (False, True)):
                q = qk[pl.ds(s, t), 0:GROUP_W] * (HEAD_DIM ** -0.5)
                k = qk[pl.ds(s, t), GROUP_W:2 * GROUP_W]
                v = src[0, pl.ds(s, t), 2 * GROUP_W:3 * GROUP_W].astype(F32)
                gates = src[0, pl.ds(s, t), 4 * GROUP_W:4 * GROUP_W + LANES].astype(F32) + gb_ref[...]
                chunks.append((q, k, v, gates, rev))
            (h_f, st_f), (h_b, st_b) = _mlstm_solve(chunks, carry)
            hf[pl.ds(starts[0], t), :] = h_f
            hb[pl.ds(starts[1], t), :] = h_b
            return (st_f, st_b)

        return lax.fori_loop(0, n, body, states)

    zero = (tuple(jnp.zeros((LANES, LANES), F32) for _ in range(N_HEADS_G // 2)),
            tuple(jnp.zeros((1, 1), F32) for _ in range(N_HEADS_G)))
    states = scan(pc_ref, qkc_ref, hfc_ref, hbc_ref, (zero, zero))
    scan(p_ref, qk_ref, hf_ref, hb_ref, states)

    def finish(src, hf, hb, out):
        ls = src.shape[1]

        def body(c, carry):
            s = pl.multiple_of(c * t, t)
            h = hf[pl.ds(s, t), :] + hb[pl.ds(s, t), :]
            o = src[0, pl.ds(s, t), 3 * GROUP_W:4 * GROUP_W]
            out[0, pl.ds(s, t), :] = _sigmoid(o) * _head_norm(h, ones_bd, ng_ref[...], nb_ref[...])
            return carry

        lax.fori_loop(0, ls // t, body, 0)

    finish(p_ref, hf_ref, hb_ref, o_ref)
    finish(pc_ref, hfc_ref, hbc_ref, oc_ref)


def _mixer_out(bsz, ls, lc):
    return ([_seq_spec(ls, GROUP_W), _seq_spec(lc, GROUP_W)],
            [jax.ShapeDtypeStruct((bsz, ls, GROUP_W), F32), jax.ShapeDtypeStruct((bsz, lc, GROUP_W), F32)])


def _mlstm_call(p, pc, conv_w, conv_b, gate_b, ng, nb):
    bsz, ls, w = p.shape
    lc = pc.shape[1]
    gb = jnp.zeros((1, LANES), F32).at[0, :4 * N_HEADS_G].set(gate_b.reshape(-1))
    consts = [conv_w, conv_b.reshape(1, -1), gb, ng.reshape(1, -1), nb.reshape(1, -1)]
    out_specs, out_shape = _mixer_out(bsz, ls, lc)
    return pl.pallas_call(
        _mlstm_kernel,
        grid=(bsz,),
        in_specs=[_seq_spec(ls, w), _seq_spec(lc, w)] + [_const_spec(a.shape) for a in consts],
        out_specs=out_specs,
        out_shape=out_shape,
        scratch_shapes=[pltpu.VMEM((ls, 2 * GROUP_W), F32), pltpu.VMEM((lc, 2 * GROUP_W), F32),
                        pltpu.VMEM((ls, GROUP_W), F32), pltpu.VMEM((ls, GROUP_W), F32),
                        pltpu.VMEM((lc, GROUP_W), F32), pltpu.VMEM((lc, GROUP_W), F32)],
        compiler_params=_cparams(("parallel",)),
        name="mlstm_mixer",
    )(p, pc, *consts)


def _ret_log_gamma(h, rev):
    hh = (N_HEADS_G - 1 - h) if rev else h
    return math.log(1.0 - 2.0 ** (-5.0 - hh))


def _ret_solve(chunks, states):
    t = chunks[0][0].shape[0]
    hd = HEAD_DIM
    lo = _iota((t, LANES), 1) < hd
    r = _iota((t, t), 0)
    cc = _iota((t, t), 1)
    work = []
    for di, (q, k, v, rev) in enumerate(chunks):
        diff = ((cc - r) if rev else (r - cc)).astype(F32)
        valid = diff >= 0.0
        ordinal = _iota((t, 1), 0).astype(F32)
        if rev:
            ordinal = (t - 1.0) - ordinal
        for p in range(N_HEADS_G // 2):
            cols = slice(LANES * p, LANES * (p + 1))
            lg = [_ret_log_gamma(2 * p + e, rev) for e in range(2)]
            work.append(dict(di=di, p=p, q=q[:, cols], k=k[:, cols], v=v[:, cols], lg=lg, diff=diff, valid=valid,
                             ordinal=ordinal, s0=states[di][p]))
    for w in work:
        kb = _bf(w["k"])
        w["qm"] = [jnp.where(lo if e == 0 else jnp.logical_not(lo), w["q"], 0.0) for e in range(2)]
        w["s_raw"] = [_dg(_bf(w["qm"][e]), kb, NT) for e in range(2)]
    for w in work:
        dm = [jnp.where(w["valid"], jnp.exp(jnp.where(w["valid"], w["diff"], 0.0) * w["lg"][e]), 0.0) for e in range(2)]
        q_dec = jnp.where(lo, jnp.exp((w["ordinal"] + 1.0) * w["lg"][0]), jnp.exp((w["ordinal"] + 1.0) * w["lg"][1]))
        k_dec = jnp.where(lo, jnp.exp((t - 1.0 - w["ordinal"]) * w["lg"][0]),
                          jnp.exp((t - 1.0 - w["ordinal"]) * w["lg"][1]))
        vb = _bf(w["v"])
        zero = jnp.zeros_like(vb)
        vm = [jnp.where(lo, vb, zero), jnp.where(lo, zero, vb)]
        lhs = jnp.concatenate([_bf(w["s_raw"][0] * dm[0]), _bf(w["s_raw"][1] * dm[1]), _bf(w["q"] * q_dec)], axis=1)
        rhs = jnp.concatenate([vm[0], vm[1], _bf(w["s0"])], axis=0)
        w["o"] = _dg(lhs, rhs)
        kd = _bf(w["k"] * k_dec)
        kz = jnp.zeros_like(kd)
        upd = _dg(jnp.concatenate([jnp.where(lo, kd, kz), jnp.where(lo, kz, kd)], axis=0),
                  jnp.concatenate(vm, axis=0), TN)
        rows = _iota((LANES, 1), 0)
        cd = jnp.where(rows < hd, math.exp(t * w["lg"][0]), math.exp(t * w["lg"][1]))
        w["s_new"] = w["s0"] * cd + upd
    out = []
    for di in range(len(chunks)):
        mine = [w for w in work if w["di"] == di]
        out.append((jnp.concatenate([w["o"] for w in mine], axis=-1), tuple(w["s_new"] for w in mine)))
    return out


def _rope(x, cos_t, sin_t):
    w = x.shape[-1]
    lane = _iota(x.shape, 1)
    partner = jnp.where((lane & 31) < ROPE_PAIRS, pltpu.roll(x, w - ROPE_PAIRS, 1), pltpu.roll(x, ROPE_PAIRS, 1))
    return x * cos_t + partner * sin_t


def _ret_kernel(p_ref, pc_ref, cos_ref, sin_ref, ng_ref, nb_ref, o_ref, oc_ref,
                hf_ref, hb_ref, hfc_ref, hbc_ref):
    t = CHUNK
    ones_bd = _head_ones()

    def scan(src, hf, hb, states, rotate):
        ls = src.shape[1]
        n = ls // t

        def body(i, carry):
            starts = (pl.multiple_of(i * t, t), pl.multiple_of((n - 1 - i) * t, t))
            chunks = []
            for s, rev in zip(starts, (False, True)):
                q = src[0, pl.ds(s, t), 0:GROUP_W].astype(F32) * (HEAD_DIM ** -0.5)
                k = src[0, pl.ds(s, t), GROUP_W:2 * GROUP_W].astype(F32)
                v = src[0, pl.ds(s, t), 2 * GROUP_W:3 * GROUP_W]
                if rotate:
                    cos_t = cos_ref[pl.ds(s, t), :]
                    sin_t = sin_ref[pl.ds(s, t), :]
                    q = _rope(q, cos_t, sin_t)
                    k = _rope(k, cos_t, sin_t)
                chunks.append((q, k, v, rev))
            (h_f, st_f), (h_b, st_b) = _ret_solve(chunks, carry)
            hf[pl.ds(starts[0], t), :] = h_f
            hb[pl.ds(starts[1], t), :] = h_b
            return (st_f, st_b)

        return lax.fori_loop(0, n, body, states)

    zero = tuple(jnp.zeros((LANES, LANES), F32) for _ in range(N_HEADS_G // 2))
    states = scan(pc_ref, hfc_ref, hbc_ref, (zero, zero), False)
    scan(p_ref, hf_ref, hb_ref, states, True)

    def finish(src, hf, hb, out):
        ls = src.shape[1]

        def body(c, carry):
            s = pl.multiple_of(c * t, t)
            h = hf[pl.ds(s, t), :] + hb[pl.ds(s, t), :]
            g = src[0, pl.ds(s, t), 3 * GROUP_W:4 * GROUP_W]
            out[0, pl.ds(s, t), :] = g * _sigmoid(g) * _head_norm(h, ones_bd, ng_ref[...], nb_ref[...])
            return carry

        lax.fori_loop(0, ls // t, body, 0)

    finish(p_ref, hf_ref, hb_ref, o_ref)
    finish(pc_ref, hfc_ref, hbc_ref, oc_ref)


def _rope_tables(ls):
    pos = np.arange(ls)
    freqs = ROPE_BASE ** (-np.arange(ROPE_PAIRS, dtype=np.float64) / ROPE_PAIRS)
    row = (pos // GRID_W).astype(np.float64)[:, None] * freqs
    col = (pos % GRID_W).astype(np.float64)[:, None] * freqs
    ang = np.concatenate([row, row, col, col], -1)
    sign = np.tile(np.concatenate([-np.ones(ROPE_PAIRS), np.ones(ROPE_PAIRS)]), 2)
    cos_t = np.tile(np.cos(ang), (1, N_HEADS_G)).astype(np.float32)
    sin_t = np.tile(np.sin(ang) * sign, (1, N_HEADS_G)).astype(np.float32)
    return cos_t, sin_t


def _ret_call(p, pc, ng, nb):
    bsz, ls, w = p.shape
    lc = pc.shape[1]
    cos_t, sin_t = _rope_tables(ls)
    consts = [cos_t, sin_t, ng.reshape(1, -1), nb.reshape(1, -1)]
    out_specs, out_shape = _mixer_out(bsz, ls, lc)
    return pl.pallas_call(
        _ret_kernel,
        grid=(bsz,),
        in_specs=[_seq_spec(ls, w), _seq_spec(lc, w)] + [_const_spec(a.shape) for a in consts],
        out_specs=out_specs,
        out_shape=out_shape,
        scratch_shapes=[pltpu.VMEM((ls, GROUP_W), F32), pltpu.VMEM((ls, GROUP_W), F32),
                        pltpu.VMEM((lc, GROUP_W), F32), pltpu.VMEM((lc, GROUP_W), F32)],
        compiler_params=_cparams(("parallel",)),
        name="retention_mixer",
    )(p, pc, *consts)


RW_INV_PASSES = 1


def _rwkv_shifted(src, s, t, mu_ref):
    ls = src.shape[1]
    w = src.shape[2]
    cur, prev, nxt = _with_neighbors(src, s, t, ls, 0, w)
    return cur + mu_ref[0:1, :] * (prev - cur) + mu_ref[1:2, :] * (nxt - cur)


def _rwkv_prep(z, prm, rev, ones_bd):
    w0_ref, w2_ref, a0_ref, a2_ref, kks_ref, ka_ref = prm
    t = z.shape[0]
    d = 1 if rev else 0
    g = GROUP_W
    r, k, v = z[:, 0:g], z[:, g:2 * g], z[:, 2 * g:3 * g]
    wlo = z[:, 3 * g:3 * g + 32]
    alo = z[:, 3 * g + 32:3 * g + 64]
    kk = k * kks_ref[...]
    kk = kk * lax.rsqrt(_head_sum(kk * kk, ones_bd) + 1e-12)
    w_pre = w0_ref[d:d + 1, :] + _mm(jnp.tanh(wlo), w2_ref[d])
    sp = jnp.maximum(-w_pre, 0.0) + jnp.log(1.0 + jnp.exp(-jnp.abs(w_pre)))
    logw = -jnp.exp(-sp - 0.5)
    ag = _sigmoid(a0_ref[d:d + 1, :] + _mm(alo, a2_ref[d]))
    kd = k * (1.0 + (ag - 1.0) * ka_ref[...])
    a = -kk
    b = kk * ag
    tri = jnp.where(_tri_mask(t, rev), 1.0, 0.0).astype(BF16)
    cum = _mm_xl(tri, logw)
    e_pos = jnp.exp(cum)
    e_neg = jnp.exp(-cum)
    at = a * jnp.exp(cum - logw)
    rt = r * e_pos
    bt = b * e_neg
    kt = kd * e_neg
    last = 0 if rev else t - 1
    return dict(ar=_bf(jnp.concatenate([at, rt], axis=0)), bk=_bf(jnp.concatenate([bt, kt], axis=0)),
                v=_bf(v), w_last=e_pos[last:last + 1, :], rev=rev)


def _rwkv_solve(preps, states):
    t = preps[0]["v"].shape[0]
    hd = HEAD_DIM
    n_lvl = int(math.log2(t))
    row2 = _iota((t, 2 * t), 0)
    col2 = _iota((t, 2 * t), 1) & (t - 1)
    chains = []
    for di, pr in enumerate(preps):
        strict2 = (col2 > row2) if pr["rev"] else (col2 < row2)
        incl2 = (col2 >= row2) if pr["rev"] else (col2 <= row2)
        for h in range(N_HEADS_G):
            sl = slice(hd * h, hd * (h + 1))
            chains.append(dict(di=di, h=h, ar=pr["ar"][:, sl], bk=pr["bk"][:, sl], v=pr["v"][:, sl],
                               s0=states[di][h], w_last=pr["w_last"][:, sl], strict2=strict2, incl2=incl2))
    for c in chains:
        rhs = jnp.concatenate([c["bk"], _bf(c["s0"])], axis=0)
        c["gh"] = _dg(c["ar"], rhs, NT)
    for c in chains:
        gh = c["gh"]
        c["lo"] = jnp.where(c["strict2"], gh[:t, :2 * t], 0.0)
        c["m"] = _bf(jnp.where(c["incl2"], gh[t:, :2 * t], 0.0))
    for c in chains:
        c["u0"] = c["gh"][:t, 2 * t:] + _dg(_bf(c["lo"][:, t:]), c["v"])
    row = _iota((t, t), 0)
    col = _iota((t, t), 1)
    diff = row ^ col
    eye = jnp.where(diff == 0, 1.0, 0.0)
    for c in chains:
        c["l"] = c["lo"][:, :t]
        c["d"] = eye + jnp.where(diff == 1, c["l"], 0.0)
    for k in range(1, n_lvl):
        join = (diff >> k) == 1
        for c in chains:
            c["e"] = _mm(jnp.where(join, c["l"], 0.0), c["d"], passes=RW_INV_PASSES)
        for c in chains:
            c["d"] = c["d"] + _mm(c["d"], c["e"], passes=RW_INV_PASSES)
    for c in chains:
        c["uv"] = jnp.concatenate([_bf(_mm(c["d"], c["u0"], passes=RW_INV_PASSES)), c["v"]], axis=0)
    for c in chains:
        c["y"] = c["gh"][t:, 2 * t:] + _dg(c["m"], c["uv"])
    for c in chains:
        c["s_new"] = (c["s0"] + _dg(c["uv"], c["bk"], TN)) * c["w_last"]
    out = []
    for di in range(len(preps)):
        mine = [c for c in chains if c["di"] == di]
        out.append((jnp.concatenate([c["y"] for c in mine], axis=-1), tuple(c["s_new"] for c in mine)))
    return out


def _rwkv_kernel(p_ref, pc_ref, mu_ref, w0_ref, w2_ref, a0_ref, a2_ref, g2_ref, kks_ref, ka_ref, rk_ref,
                 ng_ref, nb_ref, o_ref, oc_ref, hf_ref, hb_ref, hfc_ref, hbc_ref):
    t = RWKV_CHUNK
    ones_bd = _head_ones()
    prm = (w0_ref, w2_ref, a0_ref, a2_ref, kks_ref, ka_ref)

    def scan(src, hf, hb, states):
        ls = src.shape[1]
        n = ls // t

        def body(i, carry):
            starts = (pl.multiple_of(i * t, t), pl.multiple_of((n - 1 - i) * t, t))
            preps = [_rwkv_prep(_rwkv_shifted(src, s, t, mu_ref), prm, rev, ones_bd)
                     for s, rev in zip(starts, (False, True))]
            (y_f, st_f), (y_b, st_b) = _rwkv_solve(preps, carry)
            hf[pl.ds(starts[0], t), :] = y_f
            hb[pl.ds(starts[1], t), :] = y_b
            return (st_f, st_b)

        return lax.fori_loop(0, n, body, states)

    zero = tuple(jnp.zeros((HEAD_DIM, HEAD_DIM), F32) for _ in range(N_HEADS_G))
    states = scan(pc_ref, hfc_ref, hbc_ref, (zero, zero))
    scan(p_ref, hf_ref, hb_ref, states)

    def finish(src, hf, hb, out):
        ls = src.shape[1]
        g = GROUP_W

        def body(c, carry):
            s = pl.multiple_of(c * t, t)
            z = _rwkv_shifted(src, s, t, mu_ref)
            r, k, v = z[:, 0:g], z[:, g:2 * g], z[:, 2 * g:3 * g]
            glo = z[:, 3 * g + 64:3 * g + 128]
            gate = _mm(_sigmoid(glo), g2_ref[...])
            bonus = _head_sum(r * k * rk_ref[...], ones_bd) * v
            h = hf[pl.ds(s, t), :] + hb[pl.ds(s, t), :]
            out[0, pl.ds(s, t), :] = gate * (_head_norm(h, ones_bd, ng_ref[...], nb_ref[...]) + bonus)
            return carry

        lax.fori_loop(0, ls // t, body, 0)

    finish(p_ref, hf_ref, hb_ref, o_ref)
    finish(pc_ref, hfc_ref, hbc_ref, oc_ref)


def _rwkv_call(p, pc, mu, w0, w2, a0, a2, g2, kk, ka, rk, ng, nb):
    bsz, ls, w = p.shape
    lc = pc.shape[1]
    consts = [mu, w0, w2, a0, a2, g2, kk.reshape(1, -1), ka.reshape(1, -1), rk.reshape(1, -1),
              ng.reshape(1, -1), nb.reshape(1, -1)]
    out_specs, out_shape = _mixer_out(bsz, ls, lc)
    return pl.pallas_call(
        _rwkv_kernel,
        grid=(bsz,),
        in_specs=[_seq_spec(ls, w), _seq_spec(lc, w)] + [_const_spec(a.shape) for a in consts],
        out_specs=out_specs,
        out_shape=out_shape,
        scratch_shapes=[pltpu.VMEM((ls, GROUP_W), F32), pltpu.VMEM((ls, GROUP_W), F32),
                        pltpu.VMEM((lc, GROUP_W), F32), pltpu.VMEM((lc, GROUP_W), F32)],
        compiler_params=_cparams(("parallel",)),
        name="rwkv7_mixer",
    )(p, pc, *consts)


def _dft_mats(ls):
    n = 2 * ls
    idx = np.arange(ls, dtype=np.int64)
    prod = (idx[:, None] * idx[None, :]) % n
    ang = 2.0 * np.pi * prod.astype(np.float64) / n
    return np.cos(ang), -np.sin(ang)


def _hyena_feats(ls):
    pos = np.arange(ls, dtype=np.float64)
    tt = np.linspace(0.0, 1.0, ls)[:, None]
    ang = 2.0 * math.pi * pos[:, None] / ls
    bands = np.linspace(1e-4, HYENA_BANDS - 1, HYENA_BANDS)[None, :]
    feats = np.concatenate([tt, np.cos(bands * ang), -np.sin(bands * ang)], -1)
    feats = np.pad(feats, ((0, 0), (0, LANES - HYENA_EMB))).astype(np.float32)
    max_decay = math.log(HYENA_TARGET) / HYENA_FAST_DECAY
    min_decay = math.log(HYENA_TARGET) / HYENA_SLOW_DECAY
    deltas = np.abs(np.linspace(min_decay, max_decay, GROUP_W))
    window = np.exp(-tt * deltas).astype(np.float32)
    return feats, window


def _hyena_taps_kernel(f_ref, win_ref, w1_ref, b1_ref, w2_ref, b2_ref, w3_ref, fr_ref, sum_ref, dif_ref, nyq_ref):
    fr = fr_ref[...]
    h = jnp.sin(fr * (_mm(f_ref[...], w1_ref[...], passes=3) + b1_ref[...]))
    h = jnp.sin(fr * (_mm(h, w2_ref[...], passes=3) + b2_ref[...]))
    h = _mm(h, w3_ref[...], passes=3)
    win = win_ref[...]
    win2 = jnp.concatenate([win, win], axis=-1)
    hf = h[:, 0:2 * GROUP_W] * win2
    hb = h[:, 2 * GROUP_W:4 * GROUP_W] * win2
    sum_ref[...] = hf + hb
    dif_ref[...] = hf - hb
    ls = h.shape[0]
    alt = 1.0 - 2.0 * (_iota((ls, 1), 0) & 1).astype(F32)
    nyq_ref[...] = jnp.sum((hf + hb) * alt, axis=0, keepdims=True) * (1.0 / (2.0 * ls))


def _hyena_spec_kernel(fch_ref, fcl_ref, fsh_ref, fsl_ref, sum_ref, dif_ref, kre_ref, kim_ref):
    j = pl.program_id(0)
    tf = fch_ref.shape[0]
    ls = fch_ref.shape[1]
    hs_h, hs_l = _split2(sum_ref[...])
    hd_h, hd_l = _split2(dif_ref[...])
    kre = _dg(fch_ref[...], hs_h) + (_dg(fcl_ref[...], hs_h) + _dg(fch_ref[...], hs_l))
    kim = _dg(fsh_ref[...], hd_h) + (_dg(fsl_ref[...], hd_h) + _dg(fsh_ref[...], hd_l))
    f_idx = _iota((tf, 1), 0) + j * tf
    scale = jnp.where(f_idx == 0, 1.0, 2.0) * (1.0 / (2.0 * ls))
    kre_ref[...] = kre * scale
    kim_ref[...] = kim * scale


def _hyena_filters(ls, w1, b1, w2, b2, w3, freq, fmats):
    feats, window = _hyena_feats(ls)
    w1p = jnp.pad(w1, ((0, LANES - HYENA_EMB), (0, 0)))
    args = [jnp.asarray(feats), jnp.asarray(window), w1p, b1.reshape(1, -1), w2, b2.reshape(1, -1), w3,
            freq.reshape(1, -1)]
    hsum, hdif, knyq = pl.pallas_call(
        _hyena_taps_kernel,
        out_shape=[jax.ShapeDtypeStruct((ls, 2 * GROUP_W), F32)] * 2 + [jax.ShapeDtypeStruct((1, 2 * GROUP_W), F32)],
        compiler_params=pltpu.CompilerParams(vmem_limit_bytes=VMEM_LIMIT),
        name="hyena_taps",
    )(*args)
    fch, fcl, fsh, fsl = fmats
    tf = min(512, ls)
    fspec = pl.BlockSpec((tf, ls), lambda j: (j, 0))
    tspec = pl.BlockSpec((ls, 2 * GROUP_W), lambda j: (0, 0))
    ospec = pl.BlockSpec((tf, 2 * GROUP_W), lambda j: (j, 0))
    kre, kim = pl.pallas_call(
        _hyena_spec_kernel,
        grid=(ls // tf,),
        in_specs=[fspec, fspec, fspec, fspec, tspec, tspec],
        out_specs=[ospec, ospec],
        out_shape=[jax.ShapeDtypeStruct((ls, 2 * GROUP_W), F32)] * 2,
        compiler_params=_cparams(("arbitrary",)),
        name="hyena_spectra",
    )(fch, fcl, fsh, fsl, hsum, hdif)
    return kre, kim, knyq


def _hyena_kernel(p_ref, fc_ref, fs_ref, kre_ref, kim_ref, knyq_ref, cw_ref, cb_ref, d_ref, ng_ref, nb_ref,
                  o_ref, ub_ref, yre_ref, yim_ref, z_ref, nyq_ref):
    ls = p_ref.shape[1]
    tf = min(512, ls)
    nt = ls // tf
    g = GROUP_W
    ones_bd = _head_ones()

    def alt_sign(s):
        rows = _iota((tf, 1), 0) + s
        return 1.0 - 2.0 * (rows & 1).astype(F32)

    def long_conv(order):
        c0 = order * g

        def fwd(j, carry):
            s = pl.multiple_of(j * tf, tf)
            ure = _dg(fc_ref[pl.ds(s, tf), :], ub_ref[...])
            uim = _dg(fs_ref[pl.ds(s, tf), :], ub_ref[...])
            kre = kre_ref[pl.ds(s, tf), c0:c0 + g]
            kim = kim_ref[pl.ds(s, tf), c0:c0 + g]
            yre_ref[pl.ds(s, tf), :] = _bf(ure * kre - uim * kim)
            yim_ref[pl.ds(s, tf), :] = _bf(ure * kim + uim * kre)
            return carry

        lax.fori_loop(0, nt, fwd, 0)

    def inv_tile(s, order):
        c0 = order * g
        y = _dg(fc_ref[pl.ds(s, tf), :], yre_ref[...]) + _dg(fs_ref[pl.ds(s, tf), :], yim_ref[...])
        return y + alt_sign(s) * (nyq_ref[...] * knyq_ref[0:1, c0:c0 + g])

    def prep(j, acc):
        s = pl.multiple_of(j * tf, tf)
        v = _dwconv_rows(p_ref, s, tf, ls, 0, g, cw_ref, cb_ref)
        z_ref[pl.ds(s, tf), :] = v
        ub_ref[pl.ds(s, tf), :] = _bf(v)
        return acc + jnp.sum(v * alt_sign(s), axis=0, keepdims=True)

    nyq_ref[...] = lax.fori_loop(0, nt, prep, jnp.zeros((1, g), F32))
    long_conv(0)

    def mid(j, acc):
        s = pl.multiple_of(j * tf, tf)
        v = z_ref[pl.ds(s, tf), :]
        x1 = _dwconv_rows(p_ref, s, tf, ls, g, 2 * g, cw_ref, cb_ref)
        z2 = x1 * (inv_tile(s, 0) + v * d_ref[0:1, :])
        z_ref[pl.ds(s, tf), :] = z2
        ub_ref[pl.ds(s, tf), :] = _bf(z2)
        return acc + jnp.sum(z2 * alt_sign(s), axis=0, keepdims=True)

    nyq2 = lax.fori_loop(0, nt, mid, jnp.zeros((1, g), F32))
    nyq_ref[...] = nyq2
    long_conv(1)

    def fin(j, carry):
        s = pl.multiple_of(j * tf, tf)
        z2 = z_ref[pl.ds(s, tf), :]
        x2 = _dwconv_rows(p_ref, s, tf, ls, 2 * g, 3 * g, cw_ref, cb_ref)
        y = x2 * (inv_tile(s, 1) + z2 * d_ref[1:2, :])
        o_ref[0, pl.ds(s, tf), :] = _head_norm(y, ones_bd, ng_ref[...], nb_ref[...])
        return carry

    lax.fori_loop(0, nt, fin, 0)


def _hyena_call(p, fc, fs, kre, kim, knyq, conv_w, conv_b, d_bias, ng, nb):
    bsz, ls, w = p.shape
    consts = [fc, fs, kre, kim, knyq, conv_w, conv_b.reshape(1, -1), d_bias, ng.reshape(1, -1), nb.reshape(1, -1)]
    return pl.pallas_call(
        _hyena_kernel,
        grid=(bsz,),
        in_specs=[_seq_spec(ls, w)] + [_const_spec(a.shape) for a in consts],
        out_specs=_seq_spec(ls, GROUP_W),
        out_shape=jax.ShapeDtypeStruct((bsz, ls, GROUP_W), F32),
        scratch_shapes=[pltpu.VMEM((ls, GROUP_W), BF16), pltpu.VMEM((ls, GROUP_W), BF16),
                        pltpu.VMEM((ls, GROUP_W), BF16), pltpu.VMEM((ls, GROUP_W), F32),
                        pltpu.VMEM((1, GROUP_W), F32)],
        compiler_params=_cparams(("parallel",)),
        name="hyena_mixer",
    )(p, *consts)


def _outproj_kernel(ya, yb, yr, yw, x_ref, gate_ref, w_ref, g_ref, b_ref, o_ref, *, alpha):
    y = None
    for i, part in enumerate((ya, yb, yr, yw)):
        c = _dg(_bf(part[0]), w_ref[i * GROUP_W:(i + 1) * GROUP_W, :])
        y = c if y is None else y + c
    xn = alpha * x_ref[0] + gate_ref[0] * y
    o_ref[0] = _ln_rows(xn) * g_ref[...] + b_ref[...]


def _outproj_call(parts, x, gate, w_bf, g, b, alpha):
    bsz, ls, d = x.shape
    tm = min(512, ls)
    row = lambda w: pl.BlockSpec((1, tm, w), lambda bb, i: (bb, i, 0))
    consts = [w_bf, g.reshape(1, -1), b.reshape(1, -1)]
    return pl.pallas_call(
        functools.partial(_outproj_kernel, alpha=alpha),
        grid=(bsz, ls // tm),
        in_specs=[row(GROUP_W)] * 4 + [row(d), _mod_spec(gate)] + [_const_spec(a.shape) for a in consts],
        out_specs=row(d),
        out_shape=jax.ShapeDtypeStruct((bsz, ls, d), F32),
        compiler_params=_cparams(("parallel", "parallel")),
        name="out_proj_ln",
    )(*parts, x, gate, *consts)


def _ffn_kernel(xp_ref, x_ref, xn_ref, sh_ref, sc_ref, gate_ref, wa_ref, wb_ref, cw_ref, cb_ref, wd_ref,
                g_ref, b_ref, o_ref, *, alpha):
    i = pl.program_id(1)
    nt = pl.num_programs(1)
    tl = x_ref.shape[1]
    h8 = SUBLANES
    xm = x_ref[0]
    xa = jnp.concatenate([xp_ref[0], xm, xn_ref[0]], axis=0)
    u = _bf(_ln_rows(xa) * (1.0 + sc_ref[0]) + sh_ref[0])
    rows = _iota((tl + 2 * h8, 1), 0)
    valid = jnp.logical_and(jnp.logical_or(rows >= h8, i > 0), jnp.logical_or(rows < tl + h8, i < nt - 1))
    um = u[h8:tl + h8]
    nj = wa_ref.shape[0]

    def body(j, acc):
        a = jnp.where(valid, _dg(u, wa_ref[j]), 0.0)
        bb = _dg(um, wb_ref[j])
        cw = cw_ref[j]
        prev = pltpu.roll(a, 1, 0)[h8:tl + h8]
        nxt = pltpu.roll(a, tl + 2 * h8 - 1, 0)[h8:tl + h8]
        ac = prev * cw[0:1] + a[h8:tl + h8] * cw[1:2] + nxt * cw[2:3] + cb_ref[j]
        hid = ac * _sigmoid(ac) * bb
        return acc + _dg(_bf(hid), wd_ref[j])

    f = lax.fori_loop(0, nj, body, jnp.zeros(xm.shape, F32), unroll=True)
    xn = alpha * xm + gate_ref[0] * f
    o_ref[0] = _ln_rows(xn) * g_ref[...] + b_ref[...]


def _ffn_call(x, shift, scale, gate, wa, wb, cw, cb, wd, g, b, alpha):
    bsz, ls, d = x.shape
    tl = min(512, ls)
    nb8 = tl // SUBLANES
    last8 = ls // SUBLANES - 1
    consts = [wa, wb, cw, cb, wd, g.reshape(1, -1), b.reshape(1, -1)]
    return pl.pallas_call(
        functools.partial(_ffn_kernel, alpha=alpha),
        grid=(bsz, ls // tl),
        in_specs=[pl.BlockSpec((1, SUBLANES, d), lambda bb, i: (bb, jnp.maximum(i * nb8 - 1, 0), 0)),
                  pl.BlockSpec((1, tl, d), lambda bb, i: (bb, i, 0)),
                  pl.BlockSpec((1, SUBLANES, d), lambda bb, i: (bb, jnp.minimum((i + 1) * nb8, last8), 0)),
                  _mod_spec(shift), _mod_spec(scale), _mod_spec(gate)] + [_const_spec(a.shape) for a in consts],
        out_specs=pl.BlockSpec((1, tl, d), lambda bb, i: (bb, i, 0)),
        out_shape=jax.ShapeDtypeStruct((bsz, ls, d), F32),
        compiler_params=_cparams(("parallel", "parallel")),
        name="conv_ffn_ln",
    )(x, x, x, shift, scale, gate, *consts)


FFN_COLS = 256


def _prep_w_in(w):
    d = w.shape[0]
    m = 4 * GROUP_W + 4 * N_HEADS_G
    pad = jnp.zeros((d, SEG_W[0] - m), w.dtype)
    return _bf(jnp.concatenate([w[:, :m], pad, w[:, m:]], axis=1))


def _prep_ffn(w_up, conv_w, conv_b, w_down):
    d, two_ff = w_up.shape
    dff = two_ff // 2
    nj = dff // FFN_COLS
    wa = _bf(w_up[:, :dff]).reshape(d, nj, FFN_COLS).transpose(1, 0, 2)
    wb = _bf(w_up[:, dff:]).reshape(d, nj, FFN_COLS).transpose(1, 0, 2)
    cw = conv_w.reshape(3, nj, FFN_COLS).transpose(1, 0, 2)
    cb = conv_b.reshape(nj, 1, FFN_COLS)
    wd = _bf(w_down).reshape(nj, FFN_COLS, d)
    return wa, wb, cw, cb, wd


@functools.lru_cache(maxsize=None)
def _dft_inputs(ls):
    out = []
    for m in _dft_mats(ls):
        hi = m.astype(np.float32).astype(BF16)
        lo = (m - hi.astype(np.float64)).astype(np.float32).astype(BF16)
        out += [hi, lo]
    return tuple(out)


def kernel(x, c, ctx, c_ctx, ada_w, ada_b, w_in, mlstm_conv_w, mlstm_conv_b, mlstm_gate_b, hyena_conv_w, hyena_conv_b, hyena_w1, hyena_b1, hyena_w2, hyena_b2, hyena_w3, hyena_freq, hyena_d, rwkv_mu, rwkv_w0, rwkv_w2, rwkv_a0, rwkv_a2, rwkv_g2, rwkv_kk, rwkv_ka, rwkv_rk, out_norm_g, out_norm_b, w_out, ln1_g, ln1_b, ffn_w_up, ffn_conv_w, ffn_conv_b, ffn_w_down, ln2_g, ln2_b):
    bsz, ls, d = x.shape
    lc = ctx.shape[1]
    depth = ada_w.shape[0]
    alpha = float((2 * depth) ** 0.25)
    g = GROUP_W

    n_rows = -(-(bsz + 1) // SUBLANES) * SUBLANES
    cc = jnp.concatenate([c, c_ctx[None, :], jnp.zeros((n_rows - bsz - 1, d), F32)], axis=0)

    dft_l = _dft_inputs(ls)
    dft_c = _dft_inputs(lc)

    xc = ctx
    for l in range(depth):
        last = l == depth - 1
        mod_all = _ada_call(cc, ada_w[l], ada_b[l])
        mod = [mod_all[:bsz, i * d:(i + 1) * d].reshape(bsz, 1, d) for i in range(6)]
        modc = [mod_all[bsz:bsz + 1, i * d:(i + 1) * d].reshape(1, 1, d) for i in range(6)]
        w_in_bf = _prep_w_in(w_in[l])
        ng, nb = out_norm_g[l], out_norm_b[l]
        gs = lambda i, t: t[i * g:(i + 1) * g]

        pa, ph, pr, pw = _inproj_call(x, mod[0], mod[1], w_in_bf)
        pa_c, ph_c, pr_c, pw_c = _inproj_call(xc, modc[0], modc[1], w_in_bf)

        ya, ya_c = _mlstm_call(pa, pa_c, mlstm_conv_w[l], mlstm_conv_b[l], mlstm_gate_b[l], gs(0, ng), gs(0, nb))
        hy = (hyena_w1[l], hyena_b1[l], hyena_w2[l], hyena_b2[l], hyena_w3[l], hyena_freq[l])
        kre, kim, knyq = _hyena_filters(ls, *hy, dft_l)
        yb = _hyena_call(ph, dft_l[0], dft_l[2], kre, kim, knyq, hyena_conv_w[l], hyena_conv_b[l], hyena_d[l],
                         gs(1, ng), gs(1, nb))
        yr, yr_c = _ret_call(pr, pr_c, gs(2, ng), gs(2, nb))
        yw, yw_c = _rwkv_call(pw, pw_c, rwkv_mu[l], rwkv_w0[l], rwkv_w2[l], rwkv_a0[l], rwkv_a2[l], rwkv_g2[l],
                              rwkv_kk[l], rwkv_ka[l], rwkv_rk[l], gs(3, ng), gs(3, nb))

        w_out_bf = _bf(w_out[l])
        ffn_w = _prep_ffn(ffn_w_up[l], ffn_conv_w[l], ffn_conv_b[l], ffn_w_down[l])
        x = _outproj_call((ya, yb, yr, yw), x, mod[2], w_out_bf, ln1_g[l], ln1_b[l], alpha)
        x = _ffn_call(x, mod[3], mod[4], mod[5], *ffn_w, ln2_g[l], ln2_b[l], alpha)
        if not last:
            kre_c, kim_c, knyq_c = _hyena_filters(lc, *hy, dft_c)
            yb_c = _hyena_call(ph_c, dft_c[0], dft_c[2], kre_c, kim_c, knyq_c, hyena_conv_w[l], hyena_conv_b[l],
                               hyena_d[l], gs(1, ng), gs(1, nb))
            xc = _outproj_call((ya_c, yb_c, yr_c, yw_c), xc, modc[2], w_out_bf, ln1_g[l], ln1_b[l], alpha)
            xc = _ffn_call(xc, modc[3], modc[4], modc[5], *ffn_w, ln2_g[l], ln2_b[l], alpha)
    return x
```

```python
import functools
import math

import numpy as np
import jax
import jax.numpy as jnp
from jax import lax
from jax.experimental import pallas as pl
from jax.experimental.pallas import tpu as pltpu

F32 = jnp.float32
BF16 = jnp.bfloat16

HEAD_DIM = 64
GROUP_W = 256
N_HEADS_G = 4
CHUNK = 128
RWKV_CHUNK = 64
GRID_W = 64
ROPE_PAIRS = 16
ROPE_BASE = 10000.0
HYENA_EMB = 33
HYENA_BANDS = 16
HYENA_FAST_DECAY = 0.3
HYENA_SLOW_DECAY = 1.5
HYENA_TARGET = 1e-2
LN_EPS = 1e-5
GN_EPS = 1e-5
LANES = 128
SUBLANES = 8
PACKED_ROWS = 16
ACT = jnp.bfloat16
MLSTM_NB = 2
RWKV_NB = 4
VMEM_LIMIT = 56 * 1024 * 1024

NN = (((1,), (0,)), ((), ()))
NT = (((1,), (1,)), ((), ()))
TN = (((0,), (0,)), ((), ()))


def _dg(a, b, dims=NN):
    return lax.dot_general(a, b, dims, preferred_element_type=F32)


def _bf(x):
    return x.astype(BF16)


def _split2(x):
    hi = _bf(x)
    return hi, _bf(x - hi.astype(F32))


def _split3(x):
    hi = _bf(x)
    r = x - hi.astype(F32)
    mid = _bf(r)
    return hi, mid, _bf(r - mid.astype(F32))


def _mm(a, b, dims=NN, passes=1):
    if passes == 1:
        return _dg(_bf(a), _bf(b), dims)
    ah, al = _split2(a)
    bh, bl = _split2(b)
    return _dg(ah, bh, dims) + (_dg(al, bh, dims) + _dg(ah, bl, dims))


def _mm_xl(a_exact, b, dims=NN):
    b1, b2, b3 = _split3(b)
    return _dg(a_exact, b1, dims) + (_dg(a_exact, b2, dims) + _dg(a_exact, b3, dims))


def _mm_xr(a, b_exact, dims=NN):
    a1, a2 = _split2(a)
    return _dg(a1, b_exact, dims) + _dg(a2, b_exact, dims)


def _iota(shape, dim):
    return lax.broadcasted_iota(jnp.int32, shape, dim)


def _tri_mask(t, rev, strict=False):
    r = _iota((t, t), 0)
    c = _iota((t, t), 1)
    if strict:
        return (c > r) if rev else (c < r)
    return (c >= r) if rev else (c <= r)


def _head_ones():
    r = _iota((GROUP_W, GROUP_W), 0)
    c = _iota((GROUP_W, GROUP_W), 1)
    return jnp.where((r >> 6) == (c >> 6), 1.0, 0.0).astype(BF16)


def _head_sum(x, ones_bd):
    return _mm_xr(x, ones_bd)


def _head_norm(y, ones_bd, g, b):
    mu = _head_sum(y, ones_bd) * (1.0 / HEAD_DIM)
    yc = y - mu
    var = _head_sum(yc * yc, ones_bd) * (1.0 / HEAD_DIM)
    return yc * lax.rsqrt(var + GN_EPS) * g + b


def _ln_rows(x):
    mu = jnp.mean(x, -1, keepdims=True)
    xc = x - mu
    var = jnp.mean(xc * xc, -1, keepdims=True)
    return xc * lax.rsqrt(var + LN_EPS)


def _sigmoid(x):
    return 1.0 / (1.0 + jnp.exp(-x))


def _log_sigmoid(x):
    return jnp.minimum(x, 0.0) - jnp.log(1.0 + jnp.exp(-jnp.abs(x)))


def _with_neighbors(ref, s, t, ls, c0, c1, bi=0):
    grp = PACKED_ROWS
    cur = ref[bi, pl.ds(s, t), c0:c1].astype(F32)
    p0 = pl.multiple_of(jnp.maximum(s - grp, 0), grp)
    n0 = pl.multiple_of(jnp.minimum(s + t, ls - grp), grp)
    prow = ref[bi, pl.ds(p0, grp), c0:c1].astype(F32)[grp - 1:grp]
    nrow = ref[bi, pl.ds(n0, grp), c0:c1].astype(F32)[0:1]
    prow = jnp.where(s > 0, prow, 0.0)
    nrow = jnp.where(s + t < ls, nrow, 0.0)
    rows = _iota((t, 1), 0)
    prev = jnp.where(rows == 0, prow, pltpu.roll(cur, 1, 0))
    nxt = jnp.where(rows == t - 1, nrow, pltpu.roll(cur, t - 1, 0))
    return cur, prev, nxt


def _dwconv_rows(ref, s, t, ls, c0, c1, w_ref, b_ref, bi=0):
    cur, prev, nxt = _with_neighbors(ref, s, t, ls, c0, c1, bi)
    return prev * w_ref[0:1, c0:c1] + cur * w_ref[1:2, c0:c1] + nxt * w_ref[2:3, c0:c1] + b_ref[0:1, c0:c1]


def _cparams(sem):
    return pltpu.CompilerParams(dimension_semantics=sem, vmem_limit_bytes=VMEM_LIMIT)


def _const_spec(shape):
    nd = len(shape)
    return pl.BlockSpec(shape, lambda *_: (0,) * nd, pipeline_mode=pl.Buffered(1))


def _seq_spec(ls, w, nbat=1, single=False):
    if single:
        return pl.BlockSpec((nbat, ls, w), lambda b: (b, 0, 0), pipeline_mode=pl.Buffered(1))
    return pl.BlockSpec((nbat, ls, w), lambda b: (b, 0, 0))


def _mix_nb(bsz, want):
    return want if bsz % want == 0 else 1


def _ada_kernel(c_ref, w_ref, b_ref, o_ref):
    c = c_ref[...]
    o_ref[...] = _mm(c * _sigmoid(c), w_ref[...], passes=3) + b_ref[...]


def _ada_call(cc, w, b):
    m, d = cc.shape
    n = w.shape[1]
    tn = 512
    return pl.pallas_call(
        _ada_kernel,
        grid=(n // tn,),
        in_specs=[pl.BlockSpec((m, d), lambda j: (0, 0)),
                  pl.BlockSpec((d, tn), lambda j: (0, j)),
                  pl.BlockSpec((1, tn), lambda j: (0, j))],
        out_specs=pl.BlockSpec((m, tn), lambda j: (0, j)),
        out_shape=jax.ShapeDtypeStruct((m, n), F32),
        compiler_params=_cparams(("arbitrary",)),
        name="ada_mod",
    )(cc, w, b.reshape(1, n))


SEG_W = (1152, 768, 1024, 896)


def _inproj_kernel(x_ref, sh_ref, sc_ref, w_ref, o1, o2, o3, o4):
    u = _ln_rows(x_ref[0]) * (1.0 + sc_ref[0]) + sh_ref[0]
    ub = _bf(u)
    off = 0
    for o in (o1, o2, o3, o4):
        n = o.shape[-1]
        o[0] = _dg(ub, w_ref[:, off:off + n]).astype(o.dtype)
        off += n


def _mod_spec(arr):
    d = arr.shape[-1]
    if arr.shape[0] == 1:
        return pl.BlockSpec((1, 1, d), lambda b, i: (0, 0, 0))
    return pl.BlockSpec((1, 1, d), lambda b, i: (b, 0, 0))


def _inproj_call(x, shift, scale, w_bf):
    bsz, ls, d = x.shape
    tm = min(512, ls)
    return pl.pallas_call(
        _inproj_kernel,
        grid=(bsz, ls // tm),
        in_specs=[pl.BlockSpec((1, tm, d), lambda b, i: (b, i, 0)),
                  _mod_spec(shift), _mod_spec(scale),
                  _const_spec(w_bf.shape)],
        out_specs=[pl.BlockSpec((1, tm, n), lambda b, i: (b, i, 0)) for n in SEG_W],
        out_shape=[jax.ShapeDtypeStruct((bsz, ls, n), ACT) for n in SEG_W],
        compiler_params=_cparams(("parallel", "parallel")),
        name="in_proj",
    )(x, shift, scale, w_bf)


def _mlstm_solve(chunks, states):
    t = chunks[0][0].shape[0]
    hd = HEAD_DIM
    lane = _iota((t, LANES), 1)
    lo = lane < hd
    chains = []
    for di, (q, k, v, gates, rev) in enumerate(chunks):
        d = 1 if rev else 0
        mask = _tri_mask(t, rev)
        tri = jnp.where(mask, 1.0, 0.0).astype(BF16)
        bc = _mm_xl(tri, _log_sigmoid(gates))
        gates_t = gates.T
        bc_t = bc.T
        last = 0 if rev else t - 1
        for h in range(N_HEADS_G):
            p, e = divmod(h, 2)
            ci, cf = 8 * d + h, 8 * d + 4 + h
            cols = slice(LANES * p, LANES * (p + 1))
            half = lo if e == 0 else jnp.logical_not(lo)
            den_lane = hd if e == 0 else 0
            chains.append(dict(
                di=di, h=h, p=p, e=e, half=half, den_lane=den_lane, mask=mask,
                q=q[:, cols], k=k[:, cols], v=v[:, cols],
                ig_col=gates[:, ci:ci + 1], ig_row=gates_t[ci:ci + 1, :],
                b_col=bc[:, cf:cf + 1], b_row=bc_t[cf:cf + 1, :], b_last=bc[last:last + 1, cf:cf + 1],
                cn=states[di][0][p], m=states[di][1][h]))
    for c in chains:
        c["qm"] = jnp.where(c["half"], c["q"], 0.0)
        c["s_raw"] = _dg(_bf(c["qm"]), _bf(c["k"]), NT)
    for c in chains:
        dlog = jnp.where(c["mask"], c["b_col"] - c["b_row"] + c["ig_row"], -jnp.inf)
        inter = c["b_col"] + c["m"]
        c["m_t"] = jnp.maximum(inter, jnp.max(dlog, axis=-1, keepdims=True))
        c["s"] = c["s_raw"] * jnp.exp(dlog - c["m_t"])
        c["w_inter"] = jnp.exp(inter - c["m_t"])
    for c in chains:
        c["v_aug"] = _bf(jnp.where(c["half"], c["v"], jnp.where(lane == c["den_lane"], 1.0, 0.0)))
        lhs = jnp.concatenate([_bf(c["s"]), _bf(c["qm"] * c["w_inter"])], axis=1)
        rhs = jnp.concatenate([c["v_aug"], _bf(c["cn"])], axis=0)
        c["res"] = _dg(lhs, rhs)
    for c in chains:
        dl = c["den_lane"]
        den = c["res"][:, dl:dl + 1]
        c["hh"] = c["res"] / jnp.maximum(jnp.abs(den), jnp.exp(-c["m_t"]))
    for c in chains:
        g_log = c["b_last"] - c["b_col"] + c["ig_col"]
        c["m_new"] = jnp.maximum(c["b_last"] + c["m"], jnp.max(g_log, axis=0, keepdims=True))
        c["kw"] = _bf(jnp.where(c["half"], c["k"], 0.0) * jnp.exp(g_log - c["m_new"]))
        c["decay"] = jnp.exp(c["b_last"] + c["m"] - c["m_new"])
    out = []
    row = _iota((LANES, 1), 0)
    for di in range(len(chunks)):
        hs, cns = [], []
        for p in range(N_HEADS_G // 2):
            ce, co = [c for c in chains if c["di"] == di and c["p"] == p]
            hs.append(jnp.where(lo, ce["hh"], co["hh"]))
            upd = _dg(jnp.concatenate([ce["kw"], co["kw"]], axis=0),
                      jnp.concatenate([ce["v_aug"], co["v_aug"]], axis=0), TN)
            cns.append(jnp.where(row < hd, ce["decay"], co["decay"]) * ce["cn"] + upd)
        ms = tuple(c["m_new"] for c in chains if c["di"] == di)
        out.append((jnp.concatenate(hs, axis=-1), (tuple(cns), ms)))
    return out


def _mlstm_kernel(p_ref, pc_ref, cw_ref, cb_ref, gb_ref, ng_ref, nb_ref, o_ref, oc_ref,
                  qk_ref, qkc_ref, hf_ref, hb_ref, hfc_ref, hbc_ref):
    t = CHUNK
    nbat = p_ref.shape[0]
    ones_bd = _head_ones()

    def conv_pass(src, dst):
        ls = src.shape[1]

        def body(c, carry):
            s = pl.multiple_of(c * t, t)
            for bi in range(nbat):
                y = _dwconv_rows(src, s, t, ls, 0, 2 * GROUP_W, cw_ref, cb_ref, bi)
                dst[bi, pl.ds(s, t), :] = (y * _sigmoid(y)).astype(dst.dtype)
            return carry

        lax.fori_loop(0, ls // t, body, 0)

    conv_pass(p_ref, qk_ref)
    conv_pass(pc_ref, qkc_ref)

    def scan(src, qk, hf, hb, states):
        ls = src.shape[1]
        n = ls // t

        def body(i, carry):
            starts = (pl.multiple_of(i * t, t), pl.multiple_of((n - 1 - i) * t, t))
            chunks = []
            for bi in range(nbat):
                for s, rev in zip(starts, (False, True)):
                    q = qk[bi, pl.ds(s, t), 0:GROUP_W].astype(F32) * (HEAD_DIM ** -0.5)
                    k = qk[bi, pl.ds(s, t), GROUP_W:2 * GROUP_W].astype(F32)
                    v = src[bi, pl.ds(s, t), 2 * GROUP_W:3 * GROUP_W].astype(F32)
                    gates = src[bi, pl.ds(s, t), 4 * GROUP_W:4 * GROUP_W + LANES].astype(F32) + gb_ref[...]
                    chunks.append((q, k, v, gates, rev))
            res = _mlstm_solve(chunks, carry)
            for bi in range(nbat):
                hf[bi, pl.ds(starts[0], t), :] = res[2 * bi][0]
                hb[bi, pl.ds(starts[1], t), :] = res[2 * bi + 1][0]
            return tuple(r[1] for r in res)

        return lax.fori_loop(0, n, body, states)

    zero = (tuple(jnp.zeros((LANES, LANES), F32) for _ in range(N_HEADS_G // 2)),
            tuple(jnp.zeros((1, 1), F32) for _ in range(N_HEADS_G)))
    states = scan(pc_ref, qkc_ref, hfc_ref, hbc_ref, (zero,) * (2 * nbat))
    scan(p_ref, qk_ref, hf_ref, hb_ref, states)

    def finish(src, hf, hb, out):
        ls = src.shape[1]

        def body(c, carry):
            s = pl.multiple_of(c * t, t)
            for bi in range(nbat):
                h = hf[bi, pl.ds(s, t), :] + hb[bi, pl.ds(s, t), :]
                o = src[bi, pl.ds(s, t), 3 * GROUP_W:4 * GROUP_W].astype(F32)
                y = _sigmoid(o) * _head_norm(h, ones_bd, ng_ref[...], nb_ref[...])
                out[bi, pl.ds(s, t), :] = y.astype(out.dtype)
            return carry

        lax.fori_loop(0, ls // t, body, 0)

    finish(p_ref, hf_ref, hb_ref, o_ref)
    finish(pc_ref, hfc_ref, hbc_ref, oc_ref)


def _mixer_out(bsz, ls, lc, nbat=1):
    return ([_seq_spec(ls, GROUP_W, nbat), _seq_spec(lc, GROUP_W, nbat)],
            [jax.ShapeDtypeStruct((bsz, ls, GROUP_W), ACT), jax.ShapeDtypeStruct((bsz, lc, GROUP_W), ACT)])


def _mlstm_call(p, pc, conv_w, conv_b, gate_b, ng, nb):
    bsz, ls, w = p.shape
    lc = pc.shape[1]
    nbat = _mix_nb(bsz, MLSTM_NB)
    gb = jnp.zeros((1, LANES), F32).at[0, :4 * N_HEADS_G].set(gate_b.reshape(-1))
    consts = [conv_w, conv_b.reshape(1, -1), gb, ng.reshape(1, -1), nb.reshape(1, -1)]
    out_specs, out_shape = _mixer_out(bsz, ls, lc, nbat)
    return pl.pallas_call(
        _mlstm_kernel,
        grid=(bsz // nbat,),
        in_specs=[_seq_spec(ls, w, nbat), _seq_spec(lc, w, nbat)] + [_const_spec(a.shape) for a in consts],
        out_specs=out_specs,
        out_shape=out_shape,
        scratch_shapes=[pltpu.VMEM((nbat, ls, 2 * GROUP_W), BF16), pltpu.VMEM((nbat, lc, 2 * GROUP_W), BF16),
                        pltpu.VMEM((nbat, ls, GROUP_W), F32), pltpu.VMEM((nbat, ls, GROUP_W), F32),
                        pltpu.VMEM((nbat, lc, GROUP_W), F32), pltpu.VMEM((nbat, lc, GROUP_W), F32)],
        compiler_params=_cparams(("parallel",)),
        name="mlstm_mixer",
    )(p, pc, *consts)


def _ret_log_gamma(h, rev):
    hh = (N_HEADS_G - 1 - h) if rev else h
    return math.log(1.0 - 2.0 ** (-5.0 - hh))


def _ret_solve(chunks, states):
    t = chunks[0][0].shape[0]
    hd = HEAD_DIM
    lo = _iota((t, LANES), 1) < hd
    r = _iota((t, t), 0)
    cc = _iota((t, t), 1)
    work = []
    for di, (q, k, v, rev) in enumerate(chunks):
        diff = ((cc - r) if rev else (r - cc)).astype(F32)
        valid = diff >= 0.0
        ordinal = _iota((t, 1), 0).astype(F32)
        if rev:
            ordinal = (t - 1.0) - ordinal
        for p in range(N_HEADS_G // 2):
            cols = slice(LANES * p, LANES * (p + 1))
            lg = [_ret_log_gamma(2 * p + e, rev) for e in range(2)]
            work.append(dict(di=di, p=p, q=q[:, cols], k=k[:, cols], v=v[:, cols], lg=lg, diff=diff, valid=valid,
                             ordinal=ordinal, s0=states[di][p]))
    for w in work:
        kb = _bf(w["k"])
        w["qm"] = [jnp.where(lo if e == 0 else jnp.logical_not(lo), w["q"], 0.0) for e in range(2)]
        w["s_raw"] = [_dg(_bf(w["qm"][e]), kb, NT) for e in range(2)]
    for w in work:
        dm = [jnp.where(w["valid"], jnp.exp(jnp.where(w["valid"], w["diff"], 0.0) * w["lg"][e]), 0.0) for e in range(2)]
        q_dec = jnp.where(lo, jnp.exp((w["ordinal"] + 1.0) * w["lg"][0]), jnp.exp((w["ordinal"] + 1.0) * w["lg"][1]))
        k_dec = jnp.where(lo, jnp.exp((t - 1.0 - w["ordinal"]) * w["lg"][0]),
                          jnp.exp((t - 1.0 - w["ordinal"]) * w["lg"][1]))
        vb = _bf(w["v"])
        zero = jnp.zeros_like(vb)
        vm = [jnp.where(lo, vb, zero), jnp.where(lo, zero, vb)]
        lhs = jnp.concatenate([_bf(w["s_raw"][0] * dm[0]), _bf(w["s_raw"][1] * dm[1]), _bf(w["q"] * q_dec)], axis=1)
        rhs = jnp.concatenate([vm[0], vm[1], _bf(w["s0"])], axis=0)
        w["o"] = _dg(lhs, rhs)
        kd = _bf(w["k"] * k_dec)
        kz = jnp.zeros_like(kd)
        upd = _dg(jnp.concatenate([jnp.where(lo, kd, kz), jnp.where(lo, kz, kd)], axis=0),
                  jnp.concatenate(vm, axis=0), TN)
        rows = _iota((LANES, 1), 0)
        cd = jnp.where(rows < hd, math.exp(t * w["lg"][0]), math.exp(t * w["lg"][1]))
        w["s_new"] = w["s0"] * cd + upd
    out = []
    for di in range(len(chunks)):
        mine = [w for w in work if w["di"] == di]
        out.append((jnp.concatenate([w["o"] for w in mine], axis=-1), tuple(w["s_new"] for w in mine)))
    return out


def _rope(x, cos_t, sin_t):
    w = x.shape[-1]
    lane = _iota(x.shape, 1)
    partner = jnp.where((lane & 31) < ROPE_PAIRS, pltpu.roll(x, w - ROPE_PAIRS, 1), pltpu.roll(x, ROPE_PAIRS, 1))
    return x * cos_t + partner * sin_t


def _ret_kernel(p_ref, pc_ref, cos_ref, sin_ref, ng_ref, nb_ref, o_ref, oc_ref,
                hf_ref, hb_ref, hfc_ref, hbc_ref):
    t = CHUNK
    ones_bd = _head_ones()

    def scan(src, hf, hb, states, rotate):
        ls = src.shape[1]
        n = ls // t

        def body(i, carry):
            starts = (pl.multiple_of(i * t, t), pl.multiple_of((n - 1 - i) * t, t))
            chunks = []
            for s, rev in zip(starts, (False, True)):
                q = src[0, pl.ds(s, t), 0:GROUP_W].astype(F32) * (HEAD_DIM ** -0.5)
                k = src[0, pl.ds(s, t), GROUP_W:2 * GROUP_W].astype(F32)
                v = src[0, pl.ds(s, t), 2 * GROUP_W:3 * GROUP_W].astype(F32)
                if rotate:
                    cos_t = cos_ref[pl.ds(s, t), :]
                    sin_t = sin_ref[pl.ds(s, t), :]
                    q = _rope(q, cos_t, sin_t)
                    k = _rope(k, cos_t, sin_t)
                chunks.append((q, k, v, rev))
            (h_f, st_f), (h_b, st_b) = _ret_solve(chunks, carry)
            hf[pl.ds(starts[0], t), :] = h_f
            hb[pl.ds(starts[1], t), :] = h_b
            return (st_f, st_b)

        return lax.fori_loop(0, n, body, states)

    zero = tuple(jnp.zeros((LANES, LANES), F32) for _ in range(N_HEADS_G // 2))
    states = scan(pc_ref, hfc_ref, hbc_ref, (zero, zero), False)
    scan(p_ref, hf_ref, hb_ref, states, True)

    def finish(src, hf, hb, out):
        ls = src.shape[1]

        def body(c, carry):
            s = pl.multiple_of(c * t, t)
            h = hf[pl.ds(s, t), :] + hb[pl.ds(s, t), :]
            g = src[0, pl.ds(s, t), 3 * GROUP_W:4 * GROUP_W].astype(F32)
            y = g * _sigmoid(g) * _head_norm(h, ones_bd, ng_ref[...], nb_ref[...])
            out[0, pl.ds(s, t), :] = y.astype(out.dtype)
            return carry

        lax.fori_loop(0, ls // t, body, 0)

    finish(p_ref, hf_ref, hb_ref, o_ref)
    finish(pc_ref, hfc_ref, hbc_ref, oc_ref)


def _rope_tables(ls):
    pos = np.arange(ls)
    freqs = ROPE_BASE ** (-np.arange(ROPE_PAIRS, dtype=np.float64) / ROPE_PAIRS)
    row = (pos // GRID_W).astype(np.float64)[:, None] * freqs
    col = (pos % GRID_W).astype(np.float64)[:, None] * freqs
    ang = np.concatenate([row, row, col, col], -1)
    sign = np.tile(np.concatenate([-np.ones(ROPE_PAIRS), np.ones(ROPE_PAIRS)]), 2)
    cos_t = np.tile(np.cos(ang), (1, N_HEADS_G)).astype(np.float32)
    sin_t = np.tile(np.sin(ang) * sign, (1, N_HEADS_G)).astype(np.float32)
    return cos_t, sin_t


def _ret_call(p, pc, ng, nb):
    bsz, ls, w = p.shape
    lc = pc.shape[1]
    cos_t, sin_t = _rope_tables(ls)
    consts = [cos_t, sin_t, ng.reshape(1, -1), nb.reshape(1, -1)]
    out_specs, out_shape = _mixer_out(bsz, ls, lc)
    return pl.pallas_call(
        _ret_kernel,
        grid=(bsz,),
        in_specs=[_seq_spec(ls, w), _seq_spec(lc, w)] + [_const_spec(a.shape) for a in consts],
        out_specs=out_specs,
        out_shape=out_shape,
        scratch_shapes=[pltpu.VMEM((ls, GROUP_W), F32), pltpu.VMEM((ls, GROUP_W), F32),
                        pltpu.VMEM((lc, GROUP_W), F32), pltpu.VMEM((lc, GROUP_W), F32)],
        compiler_params=_cparams(("parallel",)),
        name="retention_mixer",
    )(p, pc, *consts)


RW_INV_PASSES = 1


def _rwkv_shifted(src, s, t, mu_ref, bi=0):
    ls = src.shape[1]
    w = src.shape[2]
    cur, prev, nxt = _with_neighbors(src, s, t, ls, 0, w, bi)
    return cur + mu_ref[0:1, :] * (prev - cur) + mu_ref[1:2, :] * (nxt - cur)


def _rwkv_prep(z, prm, rev, ones_bd):
    w0_ref, w2_ref, a0_ref, a2_ref, kks_ref, ka_ref = prm
    t = z.shape[0]
    d = 1 if rev else 0
    g = GROUP_W
    r, k, v = z[:, 0:g], z[:, g:2 * g], z[:, 2 * g:3 * g]
    wlo = z[:, 3 * g:3 * g + 32]
    alo = z[:, 3 * g + 32:3 * g + 64]
    kk = k * kks_ref[...]
    kk = kk * lax.rsqrt(_head_sum(kk * kk, ones_bd) + 1e-12)
    w_pre = w0_ref[d:d + 1, :] + _mm(jnp.tanh(wlo), w2_ref[d])
    sp = jnp.maximum(-w_pre, 0.0) + jnp.log(1.0 + jnp.exp(-jnp.abs(w_pre)))
    logw = -jnp.exp(-sp - 0.5)
    ag = _sigmoid(a0_ref[d:d + 1, :] + _mm(alo, a2_ref[d]))
    kd = k * (1.0 + (ag - 1.0) * ka_ref[...])
    a = -kk
    b = kk * ag
    tri = jnp.where(_tri_mask(t, rev), 1.0, 0.0).astype(BF16)
    cum = _mm_xl(tri, logw)
    e_pos = jnp.exp(cum)
    e_neg = jnp.exp(-cum)
    at = a * jnp.exp(cum - logw)
    rt = r * e_pos
    bt = b * e_neg
    kt = kd * e_neg
    last = 0 if rev else t - 1
    return dict(at=_bf(at), rt=_bf(rt), bt=_bf(bt), kt=_bf(kt), v=_bf(v), w_last=e_pos[last:last + 1, :], rev=rev)


def _rwkv_solve(preps, states):
    t = preps[0]["v"].shape[0]
    hd = HEAD_DIM
    t2 = 2 * t
    n_lvl = int(math.log2(t))
    lo = _iota((t, LANES), 1) < hd
    rowi = _iota((t2, 2 * t2), 0) & (t - 1)
    colj = _iota((t2, 2 * t2), 1) & (t - 1)

    def expand(x):
        zero = jnp.zeros_like(x)
        return jnp.concatenate([jnp.where(lo, x, zero), jnp.where(lo, zero, x)], axis=0)

    work = []
    for di, pr in enumerate(preps):
        strict2 = (colj > rowi) if pr["rev"] else (colj < rowi)
        incl2 = (colj >= rowi) if pr["rev"] else (colj <= rowi)
        for p in range(N_HEADS_G // 2):
            cols = slice(LANES * p, LANES * (p + 1))
            ex = {name: expand(pr[name][:, cols]) for name in ("at", "rt", "bt", "kt", "v")}
            work.append(dict(di=di, p=p, ar=jnp.concatenate([ex["at"], ex["rt"]], axis=0),
                             bk=jnp.concatenate([ex["bt"], ex["kt"]], axis=0), v=ex["v"],
                             s0=states[di][p], w_last=pr["w_last"][:, cols], strict2=strict2, incl2=incl2))
    for w in work:
        rhs = jnp.concatenate([w["bk"], _bf(w["s0"])], axis=0)
        w["gh"] = _dg(w["ar"], rhs, NT)
    for w in work:
        gh = w["gh"]
        w["lo"] = jnp.where(w["strict2"], gh[:t2, :2 * t2], 0.0)
        w["m"] = _bf(jnp.where(w["incl2"], gh[t2:, :2 * t2], 0.0))
    for w in work:
        w["u0"] = w["gh"][:t2, 2 * t2:] + _dg(_bf(w["lo"][:, t2:]), w["v"])
    diff = _iota((t2, t2), 0) ^ _iota((t2, t2), 1)
    eye = jnp.where(diff == 0, 1.0, 0.0)
    for w in work:
        w["l"] = w["lo"][:, :t2]
        w["d"] = eye + jnp.where(diff == 1, w["l"], 0.0)
    for k in range(1, n_lvl):
        join = (diff >> k) == 1
        for w in work:
            w["e"] = _mm(jnp.where(join, w["l"], 0.0), w["d"], passes=RW_INV_PASSES)
        for w in work:
            w["d"] = w["d"] + _mm(w["d"], w["e"], passes=RW_INV_PASSES)
    for w in work:
        w["uv"] = jnp.concatenate([_bf(_mm(w["d"], w["u0"], passes=RW_INV_PASSES)), w["v"]], axis=0)
    for w in work:
        y_bd = w["gh"][t2:, 2 * t2:] + _dg(w["m"], w["uv"])
        w["y"] = y_bd[:t] + y_bd[t:]
    for w in work:
        w["s_new"] = (w["s0"] + _dg(w["uv"], w["bk"], TN)) * w["w_last"]
    out = []
    for di in range(len(preps)):
        mine = [w for w in work if w["di"] == di]
        out.append((jnp.concatenate([w["y"] for w in mine], axis=-1), tuple(w["s_new"] for w in mine)))
    return out


def _rwkv_kernel(p_ref, pc_ref, mu_ref, w0_ref, w2_ref, a0_ref, a2_ref, g2_ref, kks_ref, ka_ref, rk_ref,
                 ng_ref, nb_ref, o_ref, oc_ref, hf_ref, hb_ref, hfc_ref, hbc_ref):
    t = RWKV_CHUNK
    nbat = p_ref.shape[0]
    ones_bd = _head_ones()
    prm = (w0_ref, w2_ref, a0_ref, a2_ref, kks_ref, ka_ref)

    def scan(src, hf, hb, states):
        ls = src.shape[1]
        n = ls // t

        def body(i, carry):
            starts = (pl.multiple_of(i * t, t), pl.multiple_of((n - 1 - i) * t, t))
            preps = [_rwkv_prep(_rwkv_shifted(src, s, t, mu_ref, bi), prm, rev, ones_bd)
                     for bi in range(nbat) for s, rev in zip(starts, (False, True))]
            res = _rwkv_solve(preps, carry)
            for bi in range(nbat):
                hf[bi, pl.ds(starts[0], t), :] = res[2 * bi][0].astype(hf.dtype)
                hb[bi, pl.ds(starts[1], t), :] = res[2 * bi + 1][0].astype(hb.dtype)
            return tuple(r[1] for r in res)

        return lax.fori_loop(0, n, body, states)

    zero = tuple(jnp.zeros((LANES, LANES), F32) for _ in range(N_HEADS_G // 2))
    states = scan(pc_ref, hfc_ref, hbc_ref, (zero,) * (2 * nbat))
    scan(p_ref, hf_ref, hb_ref, states)

    def finish(src, hf, hb, out):
        ls = src.shape[1]
        g = GROUP_W

        def body(c, carry):
            s = pl.multiple_of(c * t, t)
            for bi in range(nbat):
                z = _rwkv_shifted(src, s, t, mu_ref, bi)
                r, k, v = z[:, 0:g], z[:, g:2 * g], z[:, 2 * g:3 * g]
                glo = z[:, 3 * g + 64:3 * g + 128]
                gate = _mm(_sigmoid(glo), g2_ref[...])
                bonus = _head_sum(r * k * rk_ref[...], ones_bd) * v
                h = hf[bi, pl.ds(s, t), :].astype(F32) + hb[bi, pl.ds(s, t), :].astype(F32)
                y = gate *(_head_norm(h, ones_bd, ng_ref[...], nb_ref[...]) + bonus)
                out[bi, pl.ds(s, t), :] = y.astype(out.dtype)
            return carry

        lax.fori_loop(0, ls // t, body, 0)

    finish(p_ref, hf_ref, hb_ref, o_ref)
    finish(pc_ref, hfc_ref, hbc_ref, oc_ref)


def _rwkv_call(p, pc, mu, w0, w2, a0, a2, g2, kk, ka, rk, ng, nb):
    bsz, ls, w = p.shape
    lc = pc.shape[1]
    nbat = _mix_nb(bsz, RWKV_NB)
    consts = [mu, w0, w2, a0, a2, g2, kk.reshape(1, -1), ka.reshape(1, -1), rk.reshape(1, -1),
              ng.reshape(1, -1), nb.reshape(1, -1)]
    out_specs, out_shape = _mixer_out(bsz, ls, lc, nbat)
    return pl.pallas_call(
        _rwkv_kernel,
        grid=(bsz // nbat,),
        in_specs=[_seq_spec(ls, w, nbat, single=True), _seq_spec(lc, w, nbat)] + [_const_spec(a.shape) for a in consts],
        out_specs=out_specs,
        out_shape=out_shape,
        scratch_shapes=[pltpu.VMEM((nbat, ls, GROUP_W), ACT), pltpu.VMEM((nbat, ls, GROUP_W), ACT),
                        pltpu.VMEM((nbat, lc, GROUP_W), ACT), pltpu.VMEM((nbat, lc, GROUP_W), ACT)],
        compiler_params=_cparams(("parallel",)),
        name="rwkv7_mixer",
    )(p, pc, *consts)


def _dft_mats(ls):
    n = 2 * ls
    idx = np.arange(ls, dtype=np.int64)
    prod = (idx[:, None] * idx[None, :]) % n
    ang = 2.0 * np.pi * prod.astype(np.float64) / n
    return np.cos(ang), -np.sin(ang)


def _hyena_feats(ls):
    pos = np.arange(ls, dtype=np.float64)
    tt = np.linspace(0.0, 1.0, ls)[:, None]
    ang = 2.0 * math.pi * pos[:, None] / ls
    bands = np.linspace(1e-4, HYENA_BANDS - 1, HYENA_BANDS)[None, :]
    feats = np.concatenate([tt, np.cos(bands * ang), -np.sin(bands * ang)], -1)
    feats = np.pad(feats, ((0, 0), (0, LANES - HYENA_EMB))).astype(np.float32)
    max_decay = math.log(HYENA_TARGET) / HYENA_FAST_DECAY
    min_decay = math.log(HYENA_TARGET) / HYENA_SLOW_DECAY
    deltas = np.abs(np.linspace(min_decay, max_decay, GROUP_W))
    window = np.exp(-tt * deltas).astype(np.float32)
    return feats, window


def _hyena_taps_kernel(f_ref, win_ref, w1_ref, b1_ref, w2_ref, b2_ref, w3_ref, fr_ref, sum_ref, dif_ref, nyq_ref):
    fr = fr_ref[...]
    h = jnp.sin(fr * (_mm(f_ref[...], w1_ref[...], passes=3) + b1_ref[...]))
    h = jnp.sin(fr * (_mm(h, w2_ref[...], passes=3) + b2_ref[...]))
    h = _mm(h, w3_ref[...], passes=3)
    win = win_ref[...]
    win2 = jnp.concatenate([win, win], axis=-1)
    hf = h[:, 0:2 * GROUP_W] * win2
    hb = h[:, 2 * GROUP_W:4 * GROUP_W] * win2
    sum_ref[...] = hf + hb
    dif_ref[...] = hf - hb
    ls = h.shape[0]
    alt = 1.0 - 2.0 * (_iota((ls, 1), 0) & 1).astype(F32)
    nyq_ref[...] = jnp.sum((hf + hb) * alt, axis=0, keepdims=True) * (1.0 / (2.0 * ls))


def _hyena_spec_kernel(fch_ref, fcl_ref, fsh_ref, fsl_ref, sum_ref, dif_ref, kre_ref, kim_ref):
    j = pl.program_id(0)
    tf = fch_ref.shape[0]
    ls = fch_ref.shape[1]
    hs_h, hs_l = _split2(sum_ref[...])
    hd_h, hd_l = _split2(dif_ref[...])
    kre = _dg(fch_ref[...], hs_h) + (_dg(fcl_ref[...], hs_h) + _dg(fch_ref[...], hs_l))
    kim = _dg(fsh_ref[...], hd_h) + (_dg(fsl_ref[...], hd_h) + _dg(fsh_ref[...], hd_l))
    f_idx = _iota((tf, 1), 0) + j * tf
    scale = jnp.where(f_idx == 0, 1.0, 2.0) * (1.0 / (2.0 * ls))
    kre_ref[...] = kre * scale
    kim_ref[...] = kim * scale


def _hyena_filters(ls, w1, b1, w2, b2, w3, freq, fmats):
    feats, window = _hyena_feats(ls)
    w1p = jnp.pad(w1, ((0, LANES - HYENA_EMB), (0, 0)))
    args = [jnp.asarray(feats), jnp.asarray(window), w1p, b1.reshape(1, -1), w2, b2.reshape(1, -1), w3,
            freq.reshape(1, -1)]
    hsum, hdif, knyq = pl.pallas_call(
        _hyena_taps_kernel,
        out_shape=[jax.ShapeDtypeStruct((ls, 2 * GROUP_W), F32)] * 2 + [jax.ShapeDtypeStruct((1, 2 * GROUP_W), F32)],
        compiler_params=pltpu.CompilerParams(vmem_limit_bytes=VMEM_LIMIT),
        name="hyena_taps",
    )(*args)
    fch, fcl, fsh, fsl = fmats
    tf = min(512, ls)
    fspec = pl.BlockSpec((tf, ls), lambda j: (j, 0))
    tspec = pl.BlockSpec((ls, 2 * GROUP_W), lambda j: (0, 0))
    ospec = pl.BlockSpec((tf, 2 * GROUP_W), lambda j: (j, 0))
    kre, kim = pl.pallas_call(
        _hyena_spec_kernel,
        grid=(ls // tf,),
        in_specs=[fspec, fspec, fspec, fspec, tspec, tspec],
        out_specs=[ospec, ospec],
        out_shape=[jax.ShapeDtypeStruct((ls, 2 * GROUP_W), F32)] * 2,
        compiler_params=_cparams(("arbitrary",)),
        name="hyena_spectra",
    )(fch, fcl, fsh, fsl, hsum, hdif)
    return kre, kim, knyq


def _hyena_kernel(p_ref, fc_ref, fs_ref, kre_ref, kim_ref, knyq_ref, cw_ref, cb_ref, d_ref, ng_ref, nb_ref,
                  o_ref, ub_ref, yre_ref, yim_ref, z_ref, nyq_ref):
    ls = p_ref.shape[1]
    tf = min(512, ls)
    nt = ls // tf
    g = GROUP_W
    ones_bd = _head_ones()

    def alt_sign(s):
        rows = _iota((tf, 1), 0) + s
        return 1.0 - 2.0 * (rows & 1).astype(F32)

    def long_conv(order):
        c0 = order * g

        def fwd(j, carry):
            s = pl.multiple_of(j * tf, tf)
            ure = _dg(fc_ref[pl.ds(s, tf), :], ub_ref[...])
            uim = _dg(fs_ref[pl.ds(s, tf), :], ub_ref[...])
            kre = kre_ref[pl.ds(s, tf), c0:c0 + g]
            kim = kim_ref[pl.ds(s, tf), c0:c0 + g]
            yre_ref[pl.ds(s, tf), :] = _bf(ure * kre - uim * kim)
            yim_ref[pl.ds(s, tf), :] = _bf(ure * kim + uim * kre)
            return carry

        lax.fori_loop(0, nt, fwd, 0)

    def inv_tile(s, order):
        c0 = order * g
        y = _dg(fc_ref[pl.ds(s, tf), :], yre_ref[...]) + _dg(fs_ref[pl.ds(s, tf), :], yim_ref[...])
        return y + alt_sign(s) * (nyq_ref[...] * knyq_ref[0:1, c0:c0 + g])

    def prep(j, acc):
        s = pl.multiple_of(j * tf, tf)
        v = _dwconv_rows(p_ref, s, tf, ls, 0, g, cw_ref, cb_ref)
        z_ref[pl.ds(s, tf), :] = v
        ub_ref[pl.ds(s, tf), :] = _bf(v)
        return acc + jnp.sum(v * alt_sign(s), axis=0, keepdims=True)

    nyq_ref[...] = lax.fori_loop(0, nt, prep, jnp.zeros((1, g), F32))
    long_conv(0)

    def mid(j, acc):
        s = pl.multiple_of(j * tf, tf)
        v = z_ref[pl.ds(s, tf), :]
        x1 = _dwconv_rows(p_ref, s, tf, ls, g, 2 * g, cw_ref, cb_ref)
        z2 = x1 * (inv_tile(s, 0) + v * d_ref[0:1, :])
        z_ref[pl.ds(s, tf), :] = z2
        ub_ref[pl.ds(s, tf), :] = _bf(z2)
        return acc + jnp.sum(z2 * alt_sign(s), axis=0, keepdims=True)

    nyq2 = lax.fori_loop(0, nt, mid, jnp.zeros((1, g), F32))
    nyq_ref[...] = nyq2
    long_conv(1)

    def fin(j, carry):
        s = pl.multiple_of(j * tf, tf)
        z2 = z_ref[pl.ds(s, tf), :]
        x2 = _dwconv_rows(p_ref, s, tf, ls, 2 * g, 3 * g, cw_ref, cb_ref)
        y = x2 * (inv_tile(s, 1) + z2 * d_ref[1:2, :])
        o_ref[0, pl.ds(s, tf), :] = _head_norm(y, ones_bd, ng_ref[...], nb_ref[...]).astype(o_ref.dtype)
        return carry

    lax.fori_loop(0, nt, fin, 0)


def _hyena_call(p, fc, fs, kre, kim, knyq, conv_w, conv_b, d_bias, ng, nb):
    bsz, ls, w = p.shape
    consts = [fc, fs, kre, kim, knyq, conv_w, conv_b.reshape(1, -1), d_bias, ng.reshape(1, -1), nb.reshape(1, -1)]
    return pl.pallas_call(
        _hyena_kernel,
        grid=(bsz,),
        in_specs=[_seq_spec(ls, w)] + [_const_spec(a.shape) for a in consts],
        out_specs=_seq_spec(ls, GROUP_W),
        out_shape=jax.ShapeDtypeStruct((bsz, ls, GROUP_W), ACT),
        scratch_shapes=[pltpu.VMEM((ls, GROUP_W), BF16), pltpu.VMEM((ls, GROUP_W), BF16),
                        pltpu.VMEM((ls, GROUP_W), BF16), pltpu.VMEM((ls, GROUP_W), F32),
                        pltpu.VMEM((1, GROUP_W), F32)],
        compiler_params=_cparams(("parallel",)),
        name="hyena_mixer",
    )(p, *consts)


def _outproj_kernel(ya, yb, yr, yw, x_ref, gate_ref, w_ref, g_ref, b_ref, o_ref, *, alpha):
    y = None
    for i, part in enumerate((ya, yb, yr, yw)):
        c = _dg(_bf(part[0]), w_ref[i * GROUP_W:(i + 1) * GROUP_W, :])
        y = c if y is None else y + c
    xn = alpha * x_ref[0] + gate_ref[0] * y
    o_ref[0] = _ln_rows(xn) * g_ref[...] + b_ref[...]


def _outproj_call(parts, x, gate, w_bf, g, b, alpha):
    bsz, ls, d = x.shape
    tm = min(512, ls)
    row = lambda w: pl.BlockSpec((1, tm, w), lambda bb, i: (bb, i, 0))
    consts = [w_bf, g.reshape(1, -1), b.reshape(1, -1)]
    return pl.pallas_call(
        functools.partial(_outproj_kernel, alpha=alpha),
        grid=(bsz, ls // tm),
        in_specs=[row(GROUP_W)] * 4 + [row(d), _mod_spec(gate)] + [_const_spec(a.shape) for a in consts],
        out_specs=row(d),
        out_shape=jax.ShapeDtypeStruct((bsz, ls, d), F32),
        compiler_params=_cparams(("parallel", "parallel")),
        name="out_proj_ln",
    )(*parts, x, gate, *consts)


def _ffn_kernel(xp_ref, x_ref, xn_ref, sh_ref, sc_ref, gate_ref, wa_ref, wb_ref, cw_ref, cb_ref, wd_ref,
                g_ref, b_ref, o_ref, *, alpha):
    i = pl.program_id(1)
    nt = pl.num_programs(1)
    tl = x_ref.shape[1]
    h8 = SUBLANES
    xm = x_ref[0]
    xa = jnp.concatenate([xp_ref[0], xm, xn_ref[0]], axis=0)
    u = _bf(_ln_rows(xa) * (1.0 + sc_ref[0]) + sh_ref[0])
    rows = _iota((tl + 2 * h8, 1), 0)
    valid = jnp.logical_and(jnp.logical_or(rows >= h8, i > 0), jnp.logical_or(rows < tl + h8, i < nt - 1))
    um = u[h8:tl + h8]
    nj = wa_ref.shape[0]

    def body(j, acc):
        a = jnp.where(valid, _dg(u, wa_ref[j]), 0.0)
        bb = _dg(um, wb_ref[j])
        cw = cw_ref[j]
        prev = pltpu.roll(a, 1, 0)[h8:tl + h8]
        nxt = pltpu.roll(a, tl + 2 * h8 - 1, 0)[h8:tl + h8]
        ac = prev * cw[0:1] + a[h8:tl + h8] * cw[1:2] + nxt * cw[2:3] + cb_ref[j]
        hid = ac * _sigmoid(ac) * bb
        return acc + _dg(_bf(hid), wd_ref[j])

    f = lax.fori_loop(0, nj, body, jnp.zeros(xm.shape, F32), unroll=True)
    xn = alpha * xm + gate_ref[0] * f
    o_ref[0] = _ln_rows(xn) * g_ref[...] + b_ref[...]


def _ffn_call(x, shift, scale, gate, wa, wb, cw, cb, wd, g, b, alpha):
    bsz, ls, d = x.shape
    tl = min(512, ls)
    nb8 = tl // SUBLANES
    last8 = ls // SUBLANES - 1
    consts = [wa, wb, cw, cb, wd, g.reshape(1, -1), b.reshape(1, -1)]
    return pl.pallas_call(
        functools.partial(_ffn_kernel, alpha=alpha),
        grid=(bsz, ls // tl),
        in_specs=[pl.BlockSpec((1, SUBLANES, d), lambda bb, i: (bb, jnp.maximum(i * nb8 - 1, 0), 0)),
                  pl.BlockSpec((1, tl, d), lambda bb, i: (bb, i, 0)),
                  pl.BlockSpec((1, SUBLANES, d), lambda bb, i: (bb, jnp.minimum((i + 1) * nb8, last8), 0)),
                  _mod_spec(shift), _mod_spec(scale), _mod_spec(gate)] + [_const_spec(a.shape) for a in consts],
        out_specs=pl.BlockSpec((1, tl, d), lambda bb, i: (bb, i, 0)),
        out_shape=jax.ShapeDtypeStruct((bsz, ls, d), F32),
        compiler_params=_cparams(("parallel", "parallel")),
        name="conv_ffn_ln",
    )(x, x, x, shift, scale, gate, *consts)


FFN_COLS = 256


def _prep_w_in(w):
    d = w.shape[0]
    m = 4 * GROUP_W + 4 * N_HEADS_G
    pad = jnp.zeros((d, SEG_W[0] - m), w.dtype)
    return _bf(jnp.concatenate([w[:, :m], pad, w[:, m:]], axis=1))


def _prep_ffn(w_up, conv_w, conv_b, w_down):
    d, two_ff = w_up.shape
    dff = two_ff // 2
    nj = dff // FFN_COLS
    wa = _bf(w_up[:, :dff]).reshape(d, nj, FFN_COLS).transpose(1, 0, 2)
    wb = _bf(w_up[:, dff:]).reshape(d, nj, FFN_COLS).transpose(1, 0, 2)
    cw = conv_w.reshape(3, nj, FFN_COLS).transpose(1, 0, 2)
    cb = conv_b.reshape(nj, 1, FFN_COLS)
    wd = _bf(w_down).reshape(nj, FFN_COLS, d)
    return wa, wb, cw, cb, wd


@functools.lru_cache(maxsize=None)
def _dft_inputs(ls):
    out = []
    for m in _dft_mats(ls):
        hi = m.astype(np.float32).astype(BF16)
        lo = (m - hi.astype(np.float64)).astype(np.float32).astype(BF16)
        out += [hi, lo]
    return tuple(out)


def kernel(x, c, ctx, c_ctx, ada_w, ada_b, w_in, mlstm_conv_w, mlstm_conv_b, mlstm_gate_b, hyena_conv_w, hyena_conv_b, hyena_w1, hyena_b1, hyena_w2, hyena_b2, hyena_w3, hyena_freq, hyena_d, rwkv_mu, rwkv_w0, rwkv_w2, rwkv_a0, rwkv_a2, rwkv_g2, rwkv_kk, rwkv_ka, rwkv_rk, out_norm_g, out_norm_b, w_out, ln1_g, ln1_b, ffn_w_up, ffn_conv_w, ffn_conv_b, ffn_w_down, ln2_g, ln2_b):
    bsz, ls, d = x.shape
    lc = ctx.shape[1]
    depth = ada_w.shape[0]
    alpha = float((2 * depth) ** 0.25)
    g = GROUP_W

    n_rows = -(-(bsz + 1) // SUBLANES) * SUBLANES
    cc = jnp.concatenate([c, c_ctx[None, :], jnp.zeros((n_rows - bsz - 1, d), F32)], axis=0)

    dft_l = _dft_inputs(ls)
    dft_c = _dft_inputs(lc)

    xc = ctx
    for l in range(depth):
        last = l == depth - 1
        mod_all = _ada_call(cc, ada_w[l], ada_b[l])
        mod = [mod_all[:bsz, i * d:(i + 1) * d].reshape(bsz, 1, d) for i in range(6)]
        modc = [mod_all[bsz:bsz + 1, i * d:(i + 1) * d].reshape(1, 1, d) for i in range(6)]
        w_in_bf = _prep_w_in(w_in[l])
        ng, nb = out_norm_g[l], out_norm_b[l]
        gs = lambda i, t: t[i * g:(i + 1) * g]

        pa, ph, pr, pw = _inproj_call(x, mod[0], mod[1], w_in_bf)
        pa_c, ph_c, pr_c, pw_c = _inproj_call(xc, modc[0], modc[1], w_in_bf)

        ya, ya_c = _mlstm_call(pa, pa_c, mlstm_conv_w[l], mlstm_conv_b[l], mlstm_gate_b[l], gs(0, ng), gs(0, nb))
        hy = (hyena_w1[l], hyena_b1[l], hyena_w2[l], hyena_b2[l], hyena_w3[l], hyena_freq[l])
        kre, kim, knyq = _hyena_filters(ls, *hy, dft_l)
        yb = _hyena_call(ph, dft_l[0], dft_l[2], kre, kim, knyq, hyena_conv_w[l], hyena_conv_b[l], hyena_d[l],
                         gs(1, ng), gs(1, nb))
        yr, yr_c = _ret_call(pr, pr_c, gs(2, ng), gs(2, nb))
        yw, yw_c = _rwkv_call(pw, pw_c, rwkv_mu[l], rwkv_w0[l], rwkv_w2[l], rwkv_a0[l], rwkv_a2[l], rwkv_g2[l],
                              rwkv_kk[l], rwkv_ka[l], rwkv_rk[l], gs(3, ng), gs(3, nb))

        w_out_bf = _bf(w_out[l])
        ffn_w = _prep_ffn(ffn_w_up[l], ffn_conv_w[l], ffn_conv_b[l], ffn_w_down[l])
        x = _outproj_call((ya, yb, yr, yw), x, mod[2], w_out_bf, ln1_g[l], ln1_b[l], alpha)
        x = _ffn_call(x, mod[3], mod[4], mod[5], *ffn_w, ln2_g[l], ln2_b[l], alpha)
        if not last:
            kre_c, kim_c, knyq_c = _hyena_filters(lc, *hy, dft_c)
            yb_c = _hyena_call(ph_c, dft_c[0], dft_c[2], kre_c, kim_c, knyq_c, hyena_conv_w[l], hyena_conv_b[l],
                               hyena_d[l], gs(1, ng), gs(1, nb))
            xc = _outproj_call((ya_c, yb_c, yr_c, yw_c), xc, modc[2], w_out_bf, ln1_g[l], ln1_b[l], alpha)
            xc = _ffn_call(xc, modc[3], modc[4], modc[5], *ffn_w, ln2_g[l], ln2_b[l], alpha)
    return x
```

```python
import functools
import math

import numpy as np
import jax
import jax.numpy as jnp
from jax import lax
from jax.experimental import pallas as pl
from jax.experimental.pallas import tpu as pltpu

F32 = jnp.float32
BF16 = jnp.bfloat16

HEAD_DIM = 64
GROUP_W = 256
N_HEADS_G = 4
CHUNK = 128
RWKV_CHUNK = 64
GRID_W = 64
ROPE_PAIRS = 16
ROPE_BASE = 10000.0
HYENA_EMB = 33
HYENA_BANDS = 16
HYENA_FAST_DECAY = 0.3
HYENA_SLOW_DECAY = 1.5
HYENA_TARGET = 1e-2
LN_EPS = 1e-5
GN_EPS = 1e-5
LANES = 128
SUBLANES = 8
PACKED_ROWS = 16
ACT = jnp.bfloat16
MLSTM_NB = 2
RWKV_NB = 4
VMEM_LIMIT = 56 * 1024 * 1024

NN = (((1,), (0,)), ((), ()))
NT = (((1,), (1,)), ((), ()))
TN = (((0,), (0,)), ((), ()))


def _dg(a, b, dims=NN):
    return lax.dot_general(a, b, dims, preferred_element_type=F32)


def _bf(x):
    return x.astype(BF16)


def _split2(x):
    hi = _bf(x)
    return hi, _bf(x - hi.astype(F32))


def _split3(x):
    hi = _bf(x)
    r = x - hi.astype(F32)
    mid = _bf(r)
    return hi, mid, _bf(r - mid.astype(F32))


def _mm(a, b, dims=NN, passes=1):
    if passes == 1:
        return _dg(_bf(a), _bf(b), dims)
    ah, al = _split2(a)
    bh, bl = _split2(b)
    return _dg(ah, bh, dims) + (_dg(al, bh, dims) + _dg(ah, bl, dims))


def _mm_xl(a_exact, b, dims=NN):
    b1, b2, b3 = _split3(b)
    return _dg(a_exact, b1, dims) + (_dg(a_exact, b2, dims) + _dg(a_exact, b3, dims))


def _mm_xr(a, b_exact, dims=NN):
    a1, a2 = _split2(a)
    return _dg(a1, b_exact, dims) + _dg(a2, b_exact, dims)


def _iota(shape, dim):
    return lax.broadcasted_iota(jnp.int32, shape, dim)


def _tri_mask(t, rev, strict=False):
    r = _iota((t, t), 0)
    c = _iota((t, t), 1)
    if strict:
        return (c > r) if rev else (c < r)
    return (c >= r) if rev else (c <= r)


def _head_ones():
    r = _iota((GROUP_W, GROUP_W), 0)
    c = _iota((GROUP_W, GROUP_W), 1)
    return jnp.where((r >> 6) == (c >> 6), 1.0, 0.0).astype(BF16)


def _head_sum(x, ones_bd):
    return _mm_xr(x, ones_bd)


def _head_norm(y, ones_bd, g, b):
    mu = _head_sum(y, ones_bd) * (1.0 / HEAD_DIM)
    yc = y - mu
    var = _head_sum(yc * yc, ones_bd) * (1.0 / HEAD_DIM)
    return yc * lax.rsqrt(var + GN_EPS) * g + b


def _ln_rows(x):
    mu = jnp.mean(x, -1, keepdims=True)
    xc = x - mu
    var = jnp.mean(xc * xc, -1, keepdims=True)
    return xc * lax.rsqrt(var + LN_EPS)


def _sigmoid(x):
    return 1.0 / (1.0 + jnp.exp(-x))


def _log_sigmoid(x):
    return jnp.minimum(x, 0.0) - jnp.log(1.0 + jnp.exp(-jnp.abs(x)))


def _with_neighbors(ref, s, t, ls, c0, c1, bi=0):
    grp = PACKED_ROWS
    cur = ref[bi, pl.ds(s, t), c0:c1].astype(F32)
    p0 = pl.multiple_of(jnp.maximum(s - grp, 0), grp)
    n0 = pl.multiple_of(jnp.minimum(s + t, ls - grp), grp)
    prow = ref[bi, pl.ds(p0, grp), c0:c1].astype(F32)[grp - 1:grp]
    nrow = ref[bi, pl.ds(n0, grp), c0:c1].astype(F32)[0:1]
    prow = jnp.where(s > 0, prow, 0.0)
    nrow = jnp.where(s + t < ls, nrow, 0.0)
    rows = _iota((t, 1), 0)
    prev = jnp.where(rows == 0, prow, pltpu.roll(cur, 1, 0))
    nxt = jnp.where(rows == t - 1, nrow, pltpu.roll(cur, t - 1, 0))
    return cur, prev, nxt


def _dwconv_rows(ref, s, t, ls, c0, c1, w_ref, b_ref, bi=0):
    cur, prev, nxt = _with_neighbors(ref, s, t, ls, c0, c1, bi)
    return prev * w_ref[0:1, c0:c1] + cur * w_ref[1:2, c0:c1] + nxt * w_ref[2:3, c0:c1] + b_ref[0:1, c0:c1]


def _cparams(sem):
    return pltpu.CompilerParams(dimension_semantics=sem, vmem_limit_bytes=VMEM_LIMIT)


def _const_spec(shape):
    nd = len(shape)
    return pl.BlockSpec(shape, lambda *_: (0,) * nd, pipeline_mode=pl.Buffered(1))


def _seq_spec(ls, w, nbat=1, single=False):
    if single:
        return pl.BlockSpec((nbat, ls, w), lambda b: (b, 0, 0), pipeline_mode=pl.Buffered(1))
    return pl.BlockSpec((nbat, ls, w), lambda b: (b, 0, 0))


def _mix_nb(bsz, want):
    return want if bsz % want == 0 else 1


def _ada_kernel(c_ref, w_ref, b_ref, o_ref):
    c = c_ref[...]
    o_ref[...] = _mm(c * _sigmoid(c), w_ref[...], passes=3) + b_ref[...]


def _ada_call(cc, w, b):
    m, d = cc.shape
    n = w.shape[1]
    tn = 512
    return pl.pallas_call(
        _ada_kernel,
        grid=(n // tn,),
        in_specs=[pl.BlockSpec((m, d), lambda j: (0, 0)),
                  pl.BlockSpec((d, tn), lambda j: (0, j)),
                  pl.BlockSpec((1, tn), lambda j: (0, j))],
        out_specs=pl.BlockSpec((m, tn), lambda j: (0, j)),
        out_shape=jax.ShapeDtypeStruct((m, n), F32),
        compiler_params=_cparams(("arbitrary",)),
        name="ada_mod",
    )(cc, w, b.reshape(1, n))


SEG_W = (1152, 768, 1024, 896)


def _inproj_kernel(x_ref, sh_ref, sc_ref, w_ref, o1, o2, o3, o4):
    u = _ln_rows(x_ref[0]) * (1.0 + sc_ref[0]) + sh_ref[0]
    ub = _bf(u)
    off = 0
    for o in (o1, o2, o3, o4):
        n = o.shape[-1]
        o[0] = _dg(ub, w_ref[:, off:off + n]).astype(o.dtype)
        off += n


def _mod_spec(arr):
    d = arr.shape[-1]
    if arr.shape[0] == 1:
        return pl.BlockSpec((1, 1, d), lambda b, i: (0, 0, 0))
    return pl.BlockSpec((1, 1, d), lambda b, i: (b, 0, 0))


def _inproj_call(x, shift, scale, w_bf):
    bsz, ls, d = x.shape
    tm = min(512, ls)
    return pl.pallas_call(
        _inproj_kernel,
        grid=(bsz, ls // tm),
        in_specs=[pl.BlockSpec((1, tm, d), lambda b, i: (b, i, 0)),
                  _mod_spec(shift), _mod_spec(scale),
                  _const_spec(w_bf.shape)],
        out_specs=[pl.BlockSpec((1, tm, n), lambda b, i: (b, i, 0)) for n in SEG_W],
        out_shape=[jax.ShapeDtypeStruct((bsz, ls, n), ACT) for n in SEG_W],
        compiler_params=_cparams(("parallel", "parallel")),
        name="in_proj",
    )(x, shift, scale, w_bf)


def _mlstm_solve(chunks, states):
    t = chunks[0][0].shape[0]
    hd = HEAD_DIM
    assert t == LANES
    lane = _iota((t, LANES), 1)
    rowi = _iota((t, LANES), 0)
    lo = lane < hd
    chains = []
    for di, (q, k, v, gates, rev) in enumerate(chunks):
        d = 1 if rev else 0
        mask = _tri_mask(t, rev)
        tri = jnp.where(mask, 1.0, 0.0).astype(BF16)
        bc = _mm_xl(tri, _log_sigmoid(gates))
        gates_t = gates.T
        bc_t = bc.T
        for h in range(N_HEADS_G):
            p, e = divmod(h, 2)
            ci, cf = 8 * d + h, 8 * d + 4 + h
            cols = slice(LANES * p, LANES * (p + 1))
            half = lo if e == 0 else jnp.logical_not(lo)
            b_rep = jnp.broadcast_to(bc[:, cf:cf + 1], (t, LANES))
            c_rep = jnp.broadcast_to(gates[:, ci:ci + 1], (t, LANES)) - b_rep
            chains.append(dict(
                di=di, h=h, p=p, e=e, half=half, mask=mask, rev=rev, last=0 if rev else t - 1,
                q=q[:, cols], k=k[:, cols], v=v[:, cols], b_rep=b_rep, c_rep=c_rep,
                c_row=gates_t[ci:ci + 1, :] - bc_t[cf:cf + 1, :],
                cn=states[di][0][p], m=states[di][1][h]))
    for c in chains:
        c["qm"] = jnp.where(c["half"], c["q"], 0.0)
        c["s_raw"] = _dg(_bf(c["qm"]), _bf(c["k"]), NT)
    for c in chains:
        cm = c["c_rep"]
        sft = 1
        while sft < t:
            if c["rev"]:
                prev = jnp.where(rowi < t - sft, pltpu.roll(cm, t - sft, 0), -jnp.inf)
            else:
                prev = jnp.where(rowi >= sft, pltpu.roll(cm, sft, 0), -jnp.inf)
            cm = jnp.maximum(cm, prev)
            sft *= 2
        c["big_m"] = jnp.maximum(cm, c["m"])
        c["s"] = c["s_raw"] * jnp.exp(jnp.where(c["mask"], c["c_row"] - c["big_m"], -jnp.inf))
        c["w_inter"] = jnp.exp(c["m"] - c["big_m"])
    for c in chains:
        c["v_aug"] = _bf(jnp.where(c["half"], c["v"], 1.0))
        lhs = jnp.concatenate([_bf(c["s"]), _bf(c["qm"] * c["w_inter"])], axis=1)
        rhs = jnp.concatenate([c["v_aug"], _bf(c["cn"])], axis=0)
        c["res"] = _dg(lhs, rhs)
    for c in chains:
        den = pltpu.roll(c["res"], hd, 1)
        c["hh"] = c["res"] / jnp.maximum(jnp.abs(den), jnp.exp(-(c["b_rep"] + c["big_m"])))
    for c in chains:
        last = c["last"]
        m_last = c["big_m"][last:last + 1, :]
        c["m_new"] = c["b_rep"][last:last + 1, :] + m_last
        c["kw"] = _bf(jnp.where(c["half"], c["k"], 0.0) * jnp.exp(c["c_rep"] - m_last))
        c["decay"] = jnp.exp(c["m"] - m_last)
    out = []
    row = _iota((LANES, LANES), 0)
    for di in range(len(chunks)):
        hs, cns = [], []
        for p in range(N_HEADS_G // 2):
            ce, co = [c for c in chains if c["di"] == di and c["p"] == p]
            hs.append(jnp.where(lo, ce["hh"], co["hh"]))
            upd = _dg(jnp.concatenate([ce["kw"], co["kw"]], axis=0),
                      jnp.concatenate([ce["v_aug"], co["v_aug"]], axis=0), TN)
            cns.append(jnp.where(row < hd, ce["decay"], co["decay"]) * ce["cn"] + upd)
        ms = tuple(c["m_new"] for c in chains if c["di"] == di)
        out.append((jnp.concatenate(hs, axis=-1), (tuple(cns), ms)))
    return out


def _mlstm_kernel(p_ref, pc_ref, cw_ref, cb_ref, gb_ref, ng_ref, nb_ref, o_ref, oc_ref,
                  qk_ref, qkc_ref, hf_ref, hb_ref, hfc_ref, hbc_ref):
    t = CHUNK
    nbat = p_ref.shape[0]
    ones_bd = _head_ones()

    def conv_pass(src, dst):
        ls = src.shape[1]

        def body(c, carry):
            s = pl.multiple_of(c * t, t)
            for bi in range(nbat):
                y = _dwconv_rows(src, s, t, ls, 0, 2 * GROUP_W, cw_ref, cb_ref, bi)
                dst[bi, pl.ds(s, t), :] = (y * _sigmoid(y)).astype(dst.dtype)
            return carry

        lax.fori_loop(0, ls // t, body, 0)

    conv_pass(p_ref, qk_ref)
    conv_pass(pc_ref, qkc_ref)

    def scan(src, qk, hf, hb, states):
        ls = src.shape[1]
        n = ls // t

        def body(i, carry):
            starts = (pl.multiple_of(i * t, t), pl.multiple_of((n - 1 - i) * t, t))
            chunks = []
            for bi in range(nbat):
                for s, rev in zip(starts, (False, True)):
                    q = qk[bi, pl.ds(s, t), 0:GROUP_W].astype(F32) * (HEAD_DIM ** -0.5)
                    k = qk[bi, pl.ds(s, t), GROUP_W:2 * GROUP_W].astype(F32)
                    v = src[bi, pl.ds(s, t), 2 * GROUP_W:3 * GROUP_W].astype(F32)
                    gates = src[bi, pl.ds(s, t), 4 * GROUP_W:4 * GROUP_W + LANES].astype(F32) + gb_ref[...]
                    chunks.append((q, k, v, gates, rev))
            res = _mlstm_solve(chunks, carry)
            for bi in range(nbat):
                hf[bi, pl.ds(starts[0], t), :] = res[2 * bi][0]
                hb[bi, pl.ds(starts[1], t), :] = res[2 * bi + 1][0]
            return tuple(r[1] for r in res)

        return lax.fori_loop(0, n, body, states)

    zero = (tuple(jnp.zeros((LANES, LANES), F32) for _ in range(N_HEADS_G // 2)),
            tuple(jnp.zeros((1, LANES), F32) for _ in range(N_HEADS_G)))
    states = scan(pc_ref, qkc_ref, hfc_ref, hbc_ref, (zero,) * (2 * nbat))
    scan(p_ref, qk_ref, hf_ref, hb_ref, states)

    def finish(src, hf, hb, out):
        ls = src.shape[1]

        def body(c, carry):
            s = pl.multiple_of(c * t, t)
            for bi in range(nbat):
                h = hf[bi, pl.ds(s, t), :] + hb[bi, pl.ds(s, t), :]
                o = src[bi, pl.ds(s, t), 3 * GROUP_W:4 * GROUP_W].astype(F32)
                y = _sigmoid(o) * _head_norm(h, ones_bd, ng_ref[...], nb_ref[...])
                out[bi, pl.ds(s, t), :] = y.astype(out.dtype)
            return carry

        lax.fori_loop(0, ls // t, body, 0)

    finish(p_ref, hf_ref, hb_ref, o_ref)
    finish(pc_ref, hfc_ref, hbc_ref, oc_ref)


def _mixer_out(bsz, ls, lc, nbat=1):
    return ([_seq_spec(ls, GROUP_W, nbat), _seq_spec(lc, GROUP_W, nbat)],
            [jax.ShapeDtypeStruct((bsz, ls, GROUP_W), ACT), jax.ShapeDtypeStruct((bsz, lc, GROUP_W), ACT)])


def _mlstm_call(p, pc, conv_w, conv_b, gate_b, ng, nb):
    bsz, ls, w = p.shape
    lc = pc.shape[1]
    nbat = _mix_nb(bsz, MLSTM_NB)
    gb = jnp.zeros((1, LANES), F32).at[0, :4 * N_HEADS_G].set(gate_b.reshape(-1))
    consts = [conv_w, conv_b.reshape(1, -1), gb, ng.reshape(1, -1), nb.reshape(1, -1)]
    out_specs, out_shape = _mixer_out(bsz, ls, lc, nbat)
    return pl.pallas_call(
        _mlstm_kernel,
        grid=(bsz // nbat,),
        in_specs=[_seq_spec(ls, w, nbat), _seq_spec(lc, w, nbat)] + [_const_spec(a.shape) for a in consts],
        out_specs=out_specs,
        out_shape=out_shape,
        scratch_shapes=[pltpu.VMEM((nbat, ls, 2 * GROUP_W), BF16), pltpu.VMEM((nbat, lc, 2 * GROUP_W), BF16),
                        pltpu.VMEM((nbat, ls, GROUP_W), F32), pltpu.VMEM((nbat, ls, GROUP_W), F32),
                        pltpu.VMEM((nbat, lc, GROUP_W), F32), pltpu.VMEM((nbat, lc, GROUP_W), F32)],
        compiler_params=_cparams(("parallel",)),
        name="mlstm_mixer",
    )(p, pc, *consts)


def _ret_log_gamma(h, rev):
    hh = (N_HEADS_G - 1 - h) if rev else h
    return math.log(1.0 - 2.0 ** (-5.0 - hh))


def _ret_solve(chunks, states):
    t = chunks[0][0].shape[0]
    hd = HEAD_DIM
    lo = _iota((t, LANES), 1) < hd
    r = _iota((t, t), 0)
    cc = _iota((t, t), 1)
    work = []
    for di, (q, k, v, rev) in enumerate(chunks):
        diff = ((cc - r) if rev else (r - cc)).astype(F32)
        valid = diff >= 0.0
        ordinal = _iota((t, 1), 0).astype(F32)
        if rev:
            ordinal = (t - 1.0) - ordinal
        for p in range(N_HEADS_G // 2):
            cols = slice(LANES * p, LANES * (p + 1))
            lg = [_ret_log_gamma(2 * p + e, rev) for e in range(2)]
            work.append(dict(di=di, p=p, q=q[:, cols], k=k[:, cols], v=v[:, cols], lg=lg, diff=diff, valid=valid,
                             ordinal=ordinal, s0=states[di][p]))
    for w in work:
        kb = _bf(w["k"])
        w["qm"] = [jnp.where(lo if e == 0 else jnp.logical_not(lo), w["q"], 0.0) for e in range(2)]
        w["s_raw"] = [_dg(_bf(w["qm"][e]), kb, NT) for e in range(2)]
    for w in work:
        dm = [jnp.where(w["valid"], jnp.exp(jnp.where(w["valid"], w["diff"], 0.0) * w["lg"][e]), 0.0) for e in range(2)]
        q_dec = jnp.where(lo, jnp.exp((w["ordinal"] + 1.0) * w["lg"][0]), jnp.exp((w["ordinal"] + 1.0) * w["lg"][1]))
        k_dec = jnp.where(lo, jnp.exp((t - 1.0 - w["ordinal"]) * w["lg"][0]),
                          jnp.exp((t - 1.0 - w["ordinal"]) * w["lg"][1]))
        vb = _bf(w["v"])
        zero = jnp.zeros_like(vb)
        vm = [jnp.where(lo, vb, zero), jnp.where(lo, zero, vb)]
        lhs = jnp.concatenate([_bf(w["s_raw"][0] * dm[0]), _bf(w["s_raw"][1] * dm[1]), _bf(w["q"] * q_dec)], axis=1)
        rhs = jnp.concatenate([vm[0], vm[1], _bf(w["s0"])], axis=0)
        w["o"] = _dg(lhs, rhs)
        kd = _bf(w["k"] * k_dec)
        kz = jnp.zeros_like(kd)
        upd = _dg(jnp.concatenate([jnp.where(lo, kd, kz), jnp.where(lo, kz, kd)], axis=0),
                  jnp.concatenate(vm, axis=0), TN)
        rows = _iota((LANES, 1), 0)
        cd = jnp.where(rows < hd, math.exp(t * w["lg"][0]), math.exp(t * w["lg"][1]))
        w["s_new"] = w["s0"] * cd + upd
    out = []
    for di in range(len(chunks)):
        mine = [w for w in work if w["di"] == di]
        out.append((jnp.concatenate([w["o"] for w in mine], axis=-1), tuple(w["s_new"] for w in mine)))
    return out


def _rope(x, cos_t, sin_t):
    w = x.shape[-1]
    lane = _iota(x.shape, 1)
    partner = jnp.where((lane & 31) < ROPE_PAIRS, pltpu.roll(x, w - ROPE_PAIRS, 1), pltpu.roll(x, ROPE_PAIRS, 1))
    return x * cos_t + partner * sin_t


def _ret_kernel(p_ref, pc_ref, cos_ref, sin_ref, ng_ref, nb_ref, o_ref, oc_ref,
                hf_ref, hb_ref, hfc_ref, hbc_ref):
    t = CHUNK
    ones_bd = _head_ones()

    def scan(src, hf, hb, states, rotate):
        ls = src.shape[1]
        n = ls // t

        def body(i, carry):
            starts = (pl.multiple_of(i * t, t), pl.multiple_of((n - 1 - i) * t, t))
            chunks = []
            for s, rev in zip(starts, (False, True)):
                q = src[0, pl.ds(s, t), 0:GROUP_W].astype(F32) * (HEAD_DIM ** -0.5)
                k = src[0, pl.ds(s, t), GROUP_W:2 * GROUP_W].astype(F32)
                v = src[0, pl.ds(s, t), 2 * GROUP_W:3 * GROUP_W].astype(F32)
                if rotate:
                    cos_t = cos_ref[pl.ds(s, t), :]
                    sin_t = sin_ref[pl.ds(s, t), :]
                    q = _rope(q, cos_t, sin_t)
                    k = _rope(k, cos_t, sin_t)
                chunks.append((q, k, v, rev))
            (h_f, st_f), (h_b, st_b) = _ret_solve(chunks, carry)
            hf[pl.ds(starts[0], t), :] = h_f
            hb[pl.ds(starts[1], t), :] = h_b
            return (st_f, st_b)

        return lax.fori_loop(0, n, body, states)

    zero = tuple(jnp.zeros((LANES, LANES), F32) for _ in range(N_HEADS_G // 2))
    states = scan(pc_ref, hfc_ref, hbc_ref, (zero, zero), False)
    scan(p_ref, hf_ref, hb_ref, states, True)

    def finish(src, hf, hb, out):
        ls = src.shape[1]

        def body(c, carry):
            s = pl.multiple_of(c * t, t)
            h = hf[pl.ds(s, t), :] + hb[pl.ds(s, t), :]
            g = src[0, pl.ds(s, t), 3 * GROUP_W:4 * GROUP_W].astype(F32)
            y = g * _sigmoid(g) * _head_norm(h, ones_bd, ng_ref[...], nb_ref[...])
            out[0, pl.ds(s, t), :] = y.astype(out.dtype)
            return carry

        lax.fori_loop(0, ls // t, body, 0)

    finish(p_ref, hf_ref, hb_ref, o_ref)
    finish(pc_ref, hfc_ref, hbc_ref, oc_ref)


def _rope_tables(ls):
    pos = np.arange(ls)
    freqs = ROPE_BASE ** (-np.arange(ROPE_PAIRS, dtype=np.float64) / ROPE_PAIRS)
    row = (pos // GRID_W).astype(np.float64)[:, None] * freqs
    col = (pos % GRID_W).astype(np.float64)[:, None] * freqs
    ang = np.concatenate([row, row, col, col], -1)
    sign = np.tile(np.concatenate([-np.ones(ROPE_PAIRS), np.ones(ROPE_PAIRS)]), 2)
    cos_t = np.tile(np.cos(ang), (1, N_HEADS_G)).astype(np.float32)
    sin_t = np.tile(np.sin(ang) * sign, (1, N_HEADS_G)).astype(np.float32)
    return cos_t, sin_t


def _ret_call(p, pc, ng, nb):
    bsz, ls, w = p.shape
    lc = pc.shape[1]
    cos_t, sin_t = _rope_tables(ls)
    consts = [cos_t, sin_t, ng.reshape(1, -1), nb.reshape(1, -1)]
    out_specs, out_shape = _mixer_out(bsz, ls, lc)
    return pl.pallas_call(
        _ret_kernel,
        grid=(bsz,),
        in_specs=[_seq_spec(ls, w), _seq_spec(lc, w)] + [_const_spec(a.shape) for a in consts],
        out_specs=out_specs,
        out_shape=out_shape,
        scratch_shapes=[pltpu.VMEM((ls, GROUP_W), F32), pltpu.VMEM((ls, GROUP_W), F32),
                        pltpu.VMEM((lc, GROUP_W), F32), pltpu.VMEM((lc, GROUP_W), F32)],
        compiler_params=_cparams(("parallel",)),
        name="retention_mixer",
    )(p, pc, *consts)


RW_INV_PASSES = 1


def _rwkv_shifted(src, s, t, mu_ref, bi=0):
    ls = src.shape[1]
    w = src.shape[2]
    cur, prev, nxt = _with_neighbors(src, s, t, ls, 0, w, bi)
    return cur + mu_ref[0:1, :] * (prev - cur) + mu_ref[1:2, :] * (nxt - cur)


def _rwkv_prep(z, prm, rev, ones_bd):
    w0_ref, w2_ref, a0_ref, a2_ref, kks_ref, ka_ref = prm
    t = z.shape[0]
    d = 1 if rev else 0
    g = GROUP_W
    r, k, v = z[:, 0:g], z[:, g:2 * g], z[:, 2 * g:3 * g]
    wlo = z[:, 3 * g:3 * g + 32]
    alo = z[:, 3 * g + 32:3 * g + 64]
    kk = k * kks_ref[...]
    kk = kk * lax.rsqrt(_head_sum(kk * kk, ones_bd) + 1e-12)
    w_pre = w0_ref[d:d + 1, :] + _mm(jnp.tanh(wlo), w2_ref[d])
    sp = jnp.maximum(-w_pre, 0.0) + jnp.log(1.0 + jnp.exp(-jnp.abs(w_pre)))
    logw = -jnp.exp(-sp - 0.5)
    ag = _sigmoid(a0_ref[d:d + 1, :] + _mm(alo, a2_ref[d]))
    kd = k * (1.0 + (ag - 1.0) * ka_ref[...])
    a = -kk
    b = kk * ag
    tri = jnp.where(_tri_mask(t, rev), 1.0, 0.0).astype(BF16)
    cum = _mm_xl(tri, logw)
    e_pos = jnp.exp(cum)
    e_neg = jnp.exp(-cum)
    at = a * jnp.exp(cum - logw)
    rt = r * e_pos
    bt = b * e_neg
    kt = kd * e_neg
    last = 0 if rev else t - 1
    return dict(at=_bf(at), rt=_bf(rt), bt=_bf(bt), kt=_bf(kt), v=_bf(v), w_last=e_pos[last:last + 1, :], rev=rev)


def _rwkv_solve(preps, states):
    t = preps[0]["v"].shape[0]
    hd = HEAD_DIM
    t2 = 2 * t
    n_lvl = int(math.log2(t))
    lo = _iota((t, LANES), 1) < hd
    rowi = _iota((t2, 2 * t2), 0) & (t - 1)
    colj = _iota((t2, 2 * t2), 1) & (t - 1)

    def expand(x):
        zero = jnp.zeros_like(x)
        return jnp.concatenate([jnp.where(lo, x, zero), jnp.where(lo, zero, x)], axis=0)

    work = []
    for di, pr in enumerate(preps):
        strict2 = (colj > rowi) if pr["rev"] else (colj < rowi)
        incl2 = (colj >= rowi) if pr["rev"] else (colj <= rowi)
        for p in range(N_HEADS_G // 2):
            cols = slice(LANES * p, LANES * (p + 1))
            ex = {name: expand(pr[name][:, cols]) for name in ("at", "rt", "bt", "kt", "v")}
            work.append(dict(di=di, p=p, ar=jnp.concatenate([ex["at"], ex["rt"]], axis=0),
                             bk=jnp.concatenate([ex["bt"], ex["kt"]], axis=0), v=ex["v"],
                             s0=states[di][p], w_last=pr["w_last"][:, cols], strict2=strict2, incl2=incl2))
    for w in work:
        rhs = jnp.concatenate([w["bk"], _bf(w["s0"])], axis=0)
        w["gh"] = _dg(w["ar"], rhs, NT)
    for w in work:
        gh = w["gh"]
        w["lo"] = jnp.where(w["strict2"], gh[:t2, :2 * t2], 0.0)
        w["m"] = _bf(jnp.where(w["incl2"], gh[t2:, :2 * t2], 0.0))
    for w in work:
        w["u0"] = w["gh"][:t2, 2 * t2:] + _dg(_bf(w["lo"][:, t2:]), w["v"])
    diff = _iota((t2, t2), 0) ^ _iota((t2, t2), 1)
    eye = jnp.where(diff == 0, 1.0, 0.0)
    for w in work:
        w["l"] = w["lo"][:, :t2]
        w["d"] = eye + jnp.where(diff == 1, w["l"], 0.0)
    for k in range(1, n_lvl):
        join = (diff >> k) == 1
        for w in work:
            w["e"] = _mm(jnp.where(join, w["l"], 0.0), w["d"], passes=RW_INV_PASSES)
        for w in work:
            w["d"] = w["d"] + _mm(w["d"], w["e"], passes=RW_INV_PASSES)
    for w in work:
        w["uv"] = jnp.concatenate([_bf(_mm(w["d"], w["u0"], passes=RW_INV_PASSES)), w["v"]], axis=0)
    for w in work:
        y_bd = w["gh"][t2:, 2 * t2:] + _dg(w["m"], w["uv"])
        w["y"] = y_bd[:t] + y_bd[t:]
    for w in work:
        w["s_new"] = (w["s0"] + _dg(w["uv"], w["bk"], TN)) * w["w_last"]
    out = []
    for di in range(len(preps)):
        mine = [w for w in work if w["di"] == di]
        out.append((jnp.concatenate([w["y"] for w in mine], axis=-1), tuple(w["s_new"] for w in mine)))
    return out


def _rwkv_kernel(p_ref, pc_ref, mu_ref, w0_ref, w2_ref, a0_ref, a2_ref, g2_ref, kks_ref, ka_ref, rk_ref,
                 ng_ref, nb_ref, o_ref, oc_ref, hf_ref, hb_ref, hfc_ref, hbc_ref):
    t = RWKV_CHUNK
    nbat = p_ref.shape[0]
    ones_bd = _head_ones()
    prm = (w0_ref, w2_ref, a0_ref, a2_ref, kks_ref, ka_ref)

    def scan(src, hf, hb, states):
        ls = src.shape[1]
        n = ls // t

        def body(i, carry):
            starts = (pl.multiple_of(i * t, t), pl.multiple_of((n - 1 - i) * t, t))
            preps = [_rwkv_prep(_rwkv_shifted(src, s, t, mu_ref, bi), prm, rev, ones_bd)
                     for bi in range(nbat) for s, rev in zip(starts, (False, True))]
            res = _rwkv_solve(preps, carry)
            for bi in range(nbat):
                hf[bi, pl.ds(starts[0], t), :] = res[2 * bi][0].astype(hf.dtype)
                hb[bi, pl.ds(starts[1], t), :] = res[2 * bi + 1][0].astype(hb.dtype)
            return tuple(r[1] for r in res)

        return lax.fori_loop(0, n, body, states)

    zero = tuple(jnp.zeros((LANES, LANES), F32) for _ in range(N_HEADS_G // 2))
    states = scan(pc_ref, hfc_ref, hbc_ref, (zero,) * (2 * nbat))
    scan(p_ref, hf_ref, hb_ref, states)

    def finish(src, hf, hb, out):
        ls = src.shape[1]
        g = GROUP_W

        def body(c, carry):
            s = pl.multiple_of(c * t, t)
            for bi in range(nbat):
                z = _rwkv_shifted(src, s, t, mu_ref, bi)
                r, k, v = z[:, 0:g], z[:, g:2 * g], z[:, 2 * g:3 * g]
                glo = z[:, 3 * g + 64:3 * g + 128]
                gate = _mm(_sigmoid(glo), g2_ref[...])
                bonus = _head_sum(r * k * rk_ref[...], ones_bd) * v
                h = hf[bi, pl.ds(s, t), :].astype(F32) + hb[bi, pl.ds(s, t), :].astype(F32)
                y = gate *(_head_norm(h, ones_bd, ng_ref[...], nb_ref[...]) + bonus)
                out[bi, pl.ds(s, t), :] = y.astype(out.dtype)
            return carry

        lax.fori_loop(0, ls // t, body, 0)

    finish(p_ref, hf_ref, hb_ref, o_ref)
    finish(pc_ref, hfc_ref, hbc_ref, oc_ref)


def _rwkv_call(p, pc, mu, w0, w2, a0, a2, g2, kk, ka, rk, ng, nb):
    bsz, ls, w = p.shape
    lc = pc.shape[1]
    nbat = _mix_nb(bsz, RWKV_NB)
    consts = [mu, w0, w2, a0, a2, g2, kk.reshape(1, -1), ka.reshape(1, -1), rk.reshape(1, -1),
              ng.reshape(1, -1), nb.reshape(1, -1)]
    out_specs, out_shape = _mixer_out(bsz, ls, lc, nbat)
    return pl.pallas_call(
        _rwkv_kernel,
        grid=(bsz // nbat,),
        in_specs=[_seq_spec(ls, w, nbat, single=True), _seq_spec(lc, w, nbat)] + [_const_spec(a.shape) for a in consts],
        out_specs=out_specs,
        out_shape=out_shape,
        scratch_shapes=[pltpu.VMEM((nbat, ls, GROUP_W), ACT), pltpu.VMEM((nbat, ls, GROUP_W), ACT),
                        pltpu.VMEM((nbat, lc, GROUP_W), ACT), pltpu.VMEM((nbat, lc, GROUP_W), ACT)],
        compiler_params=_cparams(("parallel",)),
        name="rwkv7_mixer",
    )(p, pc, *consts)


def _dft_mats(ls):
    n = 2 * ls
    idx = np.arange(ls, dtype=np.int64)
    prod = (idx[:, None] * idx[None, :]) % n
    ang = 2.0 * np.pi * prod.astype(np.float64) / n
    return np.cos(ang), -np.sin(ang)


def _hyena_feats(ls):
    pos = np.arange(ls, dtype=np.float64)
    tt = np.linspace(0.0, 1.0, ls)[:, None]
    ang = 2.0 * math.pi * pos[:, None] / ls
    bands = np.linspace(1e-4, HYENA_BANDS - 1, HYENA_BANDS)[None, :]
    feats = np.concatenate([tt, np.cos(bands * ang), -np.sin(bands * ang)], -1)
    feats = np.pad(feats, ((0, 0), (0, LANES - HYENA_EMB))).astype(np.float32)
    max_decay = math.log(HYENA_TARGET) / HYENA_FAST_DECAY
    min_decay = math.log(HYENA_TARGET) / HYENA_SLOW_DECAY
    deltas = np.abs(np.linspace(min_decay, max_decay, GROUP_W))
    window = np.exp(-tt * deltas).astype(np.float32)
    return feats, window


def _hyena_taps_kernel(f_ref, win_ref, w1_ref, b1_ref, w2_ref, b2_ref, w3_ref, fr_ref, sum_ref, dif_ref, nyq_ref):
    fr = fr_ref[...]
    h = jnp.sin(fr * (_mm(f_ref[...], w1_ref[...], passes=3) + b1_ref[...]))
    h = jnp.sin(fr * (_mm(h, w2_ref[...], passes=3) + b2_ref[...]))
    h = _mm(h, w3_ref[...], passes=3)
    win = win_ref[...]
    win2 = jnp.concatenate([win, win], axis=-1)
    hf = h[:, 0:2 * GROUP_W] * win2
    hb = h[:, 2 * GROUP_W:4 * GROUP_W] * win2
    sum_ref[...] = hf + hb
    dif_ref[...] = hf - hb
    ls = h.shape[0]
    alt = 1.0 - 2.0 * (_iota((ls, 1), 0) & 1).astype(F32)
    nyq_ref[...] = jnp.sum((hf + hb) * alt, axis=0, keepdims=True) * (1.0 / (2.0 * ls))


def _hyena_spec_kernel(fch_ref, fcl_ref, fsh_ref, fsl_ref, sum_ref, dif_ref, kre_ref, kim_ref):
    j = pl.program_id(0)
    tf = fch_ref.shape[0]
    ls = fch_ref.shape[1]
    hs_h, hs_l = _split2(sum_ref[...])
    hd_h, hd_l = _split2(dif_ref[...])
    kre = _dg(fch_ref[...], hs_h) + (_dg(fcl_ref[...], hs_h) + _dg(fch_ref[...], hs_l))
    kim = _dg(fsh_ref[...], hd_h) + (_dg(fsl_ref[...], hd_h) + _dg(fsh_ref[...], hd_l))
    f_idx = _iota((tf, 1), 0) + j * tf
    scale = jnp.where(f_idx == 0, 1.0, 2.0) * (1.0 / (2.0 * ls))
    kre_ref[...] = kre * scale
    kim_ref[...] = kim * scale


def _hyena_filters(ls, w1, b1, w2, b2, w3, freq, fmats):
    feats, window = _hyena_feats(ls)
    w1p = jnp.pad(w1, ((0, LANES - HYENA_EMB), (0, 0)))
    args = [jnp.asarray(feats), jnp.asarray(window), w1p, b1.reshape(1, -1), w2, b2.reshape(1, -1), w3,
            freq.reshape(1, -1)]
    hsum, hdif, knyq = pl.pallas_call(
        _hyena_taps_kernel,
        out_shape=[jax.ShapeDtypeStruct((ls, 2 * GROUP_W), F32)] * 2 + [jax.ShapeDtypeStruct((1, 2 * GROUP_W), F32)],
        compiler_params=pltpu.CompilerParams(vmem_limit_bytes=VMEM_LIMIT),
        name="hyena_taps",
    )(*args)
    fch, fcl, fsh, fsl = fmats
    tf = min(512, ls)
    fspec = pl.BlockSpec((tf, ls), lambda j: (j, 0))
    tspec = pl.BlockSpec((ls, 2 * GROUP_W), lambda j: (0, 0))
    ospec = pl.BlockSpec((tf, 2 * GROUP_W), lambda j: (j, 0))
    kre, kim = pl.pallas_call(
        _hyena_spec_kernel,
        grid=(ls // tf,),
        in_specs=[fspec, fspec, fspec, fspec, tspec, tspec],
        out_specs=[ospec, ospec],
        out_shape=[jax.ShapeDtypeStruct((ls, 2 * GROUP_W), F32)] * 2,
        compiler_params=_cparams(("arbitrary",)),
        name="hyena_spectra",
    )(fch, fcl, fsh, fsl, hsum, hdif)
    return kre, kim, knyq


def _hyena_kernel(p_ref, fc_ref, fs_ref, kre_ref, kim_ref, knyq_ref, cw_ref, cb_ref, d_ref, ng_ref, nb_ref,
                  o_ref, ub_ref, yre_ref, yim_ref, z_ref, nyq_ref):
    ls = p_ref.shape[1]
    tf = min(512, ls)
    nt = ls // tf
    g = GROUP_W
    ones_bd = _head_ones()

    def alt_sign(s):
        rows = _iota((tf, 1), 0) + s
        return 1.0 - 2.0 * (rows & 1).astype(F32)

    def long_conv(order):
        c0 = order * g

        def fwd(j, carry):
            s = pl.multiple_of(j * tf, tf)
            ure = _dg(fc_ref[pl.ds(s, tf), :], ub_ref[...])
            uim = _dg(fs_ref[pl.ds(s, tf), :], ub_ref[...])
            kre = kre_ref[pl.ds(s, tf), c0:c0 + g]
            kim = kim_ref[pl.ds(s, tf), c0:c0 + g]
            yre_ref[pl.ds(s, tf), :] = _bf(ure * kre - uim * kim)
            yim_ref[pl.ds(s, tf), :] = _bf(ure * kim + uim * kre)
            return carry

        lax.fori_loop(0, nt, fwd, 0)

    def inv_tile(s, order):
        c0 = order * g
        y = _dg(fc_ref[pl.ds(s, tf), :], yre_ref[...]) + _dg(fs_ref[pl.ds(s, tf), :], yim_ref[...])
        return y + alt_sign(s) * (nyq_ref[...] * knyq_ref[0:1, c0:c0 + g])

    def prep(j, acc):
        s = pl.multiple_of(j * tf, tf)
        v = _dwconv_rows(p_ref, s, tf, ls, 0, g, cw_ref, cb_ref)
        z_ref[pl.ds(s, tf), :] = v
        ub_ref[pl.ds(s, tf), :] = _bf(v)
        return acc + jnp.sum(v * alt_sign(s), axis=0, keepdims=True)

    nyq_ref[...] = lax.fori_loop(0, nt, prep, jnp.zeros((1, g), F32))
    long_conv(0)

    def mid(j, acc):
        s = pl.multiple_of(j * tf, tf)
        v = z_ref[pl.ds(s, tf), :]
        x1 = _dwconv_rows(p_ref, s, tf, ls, g, 2 * g, cw_ref, cb_ref)
        z2 = x1 * (inv_tile(s, 0) + v * d_ref[0:1, :])
        z_ref[pl.ds(s, tf), :] = z2
        ub_ref[pl.ds(s, tf), :] = _bf(z2)
        return acc + jnp.sum(z2 * alt_sign(s), axis=0, keepdims=True)

    nyq2 = lax.fori_loop(0, nt, mid, jnp.zeros((1, g), F32))
    nyq_ref[...] = nyq2
    long_conv(1)

    def fin(j, carry):
        s = pl.multiple_of(j * tf, tf)
        z2 = z_ref[pl.ds(s, tf), :]
        x2 = _dwconv_rows(p_ref, s, tf, ls, 2 * g, 3 * g, cw_ref, cb_ref)
        y = x2 * (inv_tile(s, 1) + z2 * d_ref[1:2, :])
        o_ref[0, pl.ds(s, tf), :] = _head_norm(y, ones_bd, ng_ref[...], nb_ref[...]).astype(o_ref.dtype)
        return carry

    lax.fori_loop(0, nt, fin, 0)


def _hyena_call(p, fc, fs, kre, kim, knyq, conv_w, conv_b, d_bias, ng, nb):
    bsz, ls, w = p.shape
    consts = [fc, fs, kre, kim, knyq, conv_w, conv_b.reshape(1, -1), d_bias, ng.reshape(1, -1), nb.reshape(1, -1)]
    return pl.pallas_call(
        _hyena_kernel,
        grid=(bsz,),
        in_specs=[_seq_spec(ls, w)] + [_const_spec(a.shape) for a in consts],
        out_specs=_seq_spec(ls, GROUP_W),
        out_shape=jax.ShapeDtypeStruct((bsz, ls, GROUP_W), ACT),
        scratch_shapes=[pltpu.VMEM((ls, GROUP_W), BF16), pltpu.VMEM((ls, GROUP_W), BF16),
                        pltpu.VMEM((ls, GROUP_W), BF16), pltpu.VMEM((ls, GROUP_W), F32),
                        pltpu.VMEM((1, GROUP_W), F32)],
        compiler_params=_cparams(("parallel",)),
        name="hyena_mixer",
    )(p, *consts)


def _outproj_kernel(ya, yb, yr, yw, x_ref, gate_ref, w_ref, g_ref, b_ref, o_ref, *, alpha):
    y = None
    for i, part in enumerate((ya, yb, yr, yw)):
        c = _dg(_bf(part[0]), w_ref[i * GROUP_W:(i + 1) * GROUP_W, :])
        y = c if y is None else y + c
    xn = alpha * x_ref[0] + gate_ref[0] * y
    o_ref[0] = _ln_rows(xn) * g_ref[...] + b_ref[...]


def _outproj_call(parts, x, gate, w_bf, g, b, alpha):
    bsz, ls, d = x.shape
    tm = min(512, ls)
    row = lambda w: pl.BlockSpec((1, tm, w), lambda bb, i: (bb, i, 0))
    consts = [w_bf, g.reshape(1, -1), b.reshape(1, -1)]
    return pl.pallas_call(
        functools.partial(_outproj_kernel, alpha=alpha),
        grid=(bsz, ls // tm),
        in_specs=[row(GROUP_W)] * 4 + [row(d), _mod_spec(gate)] + [_const_spec(a.shape) for a in consts],
        out_specs=row(d),
        out_shape=jax.ShapeDtypeStruct((bsz, ls, d), F32),
        compiler_params=_cparams(("parallel", "parallel")),
        name="out_proj_ln",
    )(*parts, x, gate, *consts)


def _ffn_kernel(xp_ref, x_ref, xn_ref, sh_ref, sc_ref, gate_ref, wa_ref, wb_ref, cw_ref, cb_ref, wd_ref,
                g_ref, b_ref, o_ref, *, alpha):
    i = pl.program_id(1)
    nt = pl.num_programs(1)
    tl = x_ref.shape[1]
    h8 = SUBLANES
    xm = x_ref[0]
    xa = jnp.concatenate([xp_ref[0], xm, xn_ref[0]], axis=0)
    u = _bf(_ln_rows(xa) * (1.0 + sc_ref[0]) + sh_ref[0])
    rows = _iota((tl + 2 * h8, 1), 0)
    valid = jnp.logical_and(jnp.logical_or(rows >= h8, i > 0), jnp.logical_or(rows < tl + h8, i < nt - 1))
    um = u[h8:tl + h8]
    nj = wa_ref.shape[0]

    def body(j, acc):
        a = jnp.where(valid, _dg(u, wa_ref[j]), 0.0)
        bb = _dg(um, wb_ref[j])
        cw = cw_ref[j]
        prev = pltpu.roll(a, 1, 0)[h8:tl + h8]
        nxt = pltpu.roll(a, tl + 2 * h8 - 1, 0)[h8:tl + h8]
        ac = prev * cw[0:1] + a[h8:tl + h8] * cw[1:2] + nxt * cw[2:3] + cb_ref[j]
        hid = ac * _sigmoid(ac) * bb
        return acc + _dg(_bf(hid), wd_ref[j])

    f = lax.fori_loop(0, nj, body, jnp.zeros(xm.shape, F32), unroll=True)
    xn = alpha * xm + gate_ref[0] * f
    o_ref[0] = _ln_rows(xn) * g_ref[...] + b_ref[...]


def _ffn_call(x, shift, scale, gate, wa, wb, cw, cb, wd, g, b, alpha):
    bsz, ls, d = x.shape
    tl = min(512, ls)
    nb8 = tl // SUBLANES
    last8 = ls // SUBLANES - 1
    consts = [wa, wb, cw, cb, wd, g.reshape(1, -1), b.reshape(1, -1)]
    return pl.pallas_call(
        functools.partial(_ffn_kernel, alpha=alpha),
        grid=(bsz, ls // tl),
        in_specs=[pl.BlockSpec((1, SUBLANES, d), lambda bb, i: (bb, jnp.maximum(i * nb8 - 1, 0), 0)),
                  pl.BlockSpec((1, tl, d), lambda bb, i: (bb, i, 0)),
                  pl.BlockSpec((1, SUBLANES, d), lambda bb, i: (bb, jnp.minimum((i + 1) * nb8, last8), 0)),
                  _mod_spec(shift), _mod_spec(scale), _mod_spec(gate)] + [_const_spec(a.shape) for a in consts],
        out_specs=pl.BlockSpec((1, tl, d), lambda bb, i: (bb, i, 0)),
        out_shape=jax.ShapeDtypeStruct((bsz, ls, d), F32),
        compiler_params=_cparams(("parallel", "parallel")),
        name="conv_ffn_ln",
    )(x, x, x, shift, scale, gate, *consts)


FFN_COLS = 256


def _prep_w_in(w):
    d = w.shape[0]
    m = 4 * GROUP_W + 4 * N_HEADS_G
    pad = jnp.zeros((d, SEG_W[0] - m), w.dtype)
    return _bf(jnp.concatenate([w[:, :m], pad, w[:, m:]], axis=1))


def _prep_ffn(w_up, conv_w, conv_b, w_down):
    d, two_ff = w_up.shape
    dff = two_ff // 2
    nj = dff // FFN_COLS
    wa = _bf(w_up[:, :dff]).reshape(d, nj, FFN_COLS).transpose(1, 0, 2)
    wb = _bf(w_up[:, dff:]).reshape(d, nj, FFN_COLS).transpose(1, 0, 2)
    cw = conv_w.reshape(3, nj, FFN_COLS).transpose(1, 0, 2)
    cb = conv_b.reshape(nj, 1, FFN_COLS)
    wd = _bf(w_down).reshape(nj, FFN_COLS, d)
    return wa, wb, cw, cb, wd


@functools.lru_cache(maxsize=None)
def _dft_inputs(ls):
    out = []
    for m in _dft_mats(ls):
        hi = m.astype(np.float32).astype(BF16)
        lo = (m - hi.astype(np.float64)).astype(np.float32).astype(BF16)
        out += [hi, lo]
    return tuple(out)


def kernel(x, c, ctx, c_ctx, ada_w, ada_b, w_in, mlstm_conv_w, mlstm_conv_b, mlstm_gate_b, hyena_conv_w, hyena_conv_b, hyena_w1, hyena_b1, hyena_w2, hyena_b2, hyena_w3, hyena_freq, hyena_d, rwkv_mu, rwkv_w0, rwkv_w2, rwkv_a0, rwkv_a2, rwkv_g2, rwkv_kk, rwkv_ka, rwkv_rk, out_norm_g, out_norm_b, w_out, ln1_g, ln1_b, ffn_w_up, ffn_conv_w, ffn_conv_b, ffn_w_down, ln2_g, ln2_b):
    bsz, ls, d = x.shape
    lc = ctx.shape[1]
    depth = ada_w.shape[0]
    alpha = float((2 * depth) ** 0.25)
    g = GROUP_W

    n_rows = -(-(bsz + 1) // SUBLANES) * SUBLANES
    cc = jnp.concatenate([c, c_ctx[None, :], jnp.zeros((n_rows - bsz - 1, d), F32)], axis=0)

    dft_l = _dft_inputs(ls)
    dft_c = _dft_inputs(lc)

    xc = ctx
    for l in range(depth):
        last = l == depth - 1
        mod_all = _ada_call(cc, ada_w[l], ada_b[l])
        mod = [mod_all[:bsz, i * d:(i + 1) * d].reshape(bsz, 1, d) for i in range(6)]
        modc = [mod_all[bsz:bsz + 1, i * d:(i + 1) * d].reshape(1, 1, d) for i in range(6)]
        w_in_bf = _prep_w_in(w_in[l])
        ng, nb = out_norm_g[l], out_norm_b[l]
        gs = lambda i, t: t[i * g:(i + 1) * g]

        pa, ph, pr, pw = _inproj_call(x, mod[0], mod[1], w_in_bf)
        pa_c, ph_c, pr_c, pw_c = _inproj_call(xc, modc[0], modc[1], w_in_bf)

        ya, ya_c = _mlstm_call(pa, pa_c, mlstm_conv_w[l], mlstm_conv_b[l], mlstm_gate_b[l], gs(0, ng), gs(0, nb))
        hy = (hyena_w1[l], hyena_b1[l], hyena_w2[l], hyena_b2[l], hyena_w3[l], hyena_freq[l])
        kre, kim, knyq = _hyena_filters(ls, *hy, dft_l)
        yb = _hyena_call(ph, dft_l[0], dft_l[2], kre, kim, knyq, hyena_conv_w[l], hyena_conv_b[l], hyena_d[l],
                         gs(1, ng), gs(1, nb))
        yr, yr_c = _ret_call(pr, pr_c, gs(2, ng), gs(2, nb))
        yw, yw_c = _rwkv_call(pw, pw_c, rwkv_mu[l], rwkv_w0[l], rwkv_w2[l], rwkv_a0[l], rwkv_a2[l], rwkv_g2[l],
                              rwkv_kk[l], rwkv_ka[l], rwkv_rk[l], gs(3, ng), gs(3, nb))

        w_out_bf = _bf(w_out[l])
        ffn_w = _prep_ffn(ffn_w_up[l], ffn_conv_w[l], ffn_conv_b[l], ffn_w_down[l])
        x = _outproj_call((ya, yb, yr, yw), x, mod[2], w_out_bf, ln1_g[l], ln1_b[l], alpha)
        x = _ffn_call(x, mod[3], mod[4], mod[5], *ffn_w, ln2_g[l], ln2_b[l], alpha)
        if not last:
            kre_c, kim_c, knyq_c = _hyena_filters(lc, *hy, dft_c)
            yb_c = _hyena_call(ph_c, dft_c[0], dft_c[2], kre_c, kim_c, knyq_c, hyena_conv_w[l], hyena_conv_b[l],
                               hyena_d[l], gs(1, ng), gs(1, nb))
            xc = _outproj_call((ya_c, yb_c, yr_c, yw_c), xc, modc[2], w_out_bf, ln1_g[l], ln1_b[l], alpha)
            xc = _ffn_call(xc, modc[3], modc[4], modc[5], *ffn_w, ln2_g[l], ln2_b[l], alpha)
    return x
```

```python
import functools
import math

import numpy as np
import jax
import jax.numpy as jnp
from jax import lax
from jax.experimental import pallas as pl
from jax.experimental.pallas import tpu as pltpu

F32 = jnp.float32
BF16 = jnp.bfloat16

HEAD_DIM = 64
GROUP_W = 256
N_HEADS_G = 4
CHUNK = 128
RWKV_CHUNK = 64
GRID_W = 64
ROPE_PAIRS = 16
ROPE_BASE = 10000.0
HYENA_EMB = 33
HYENA_BANDS = 16
HYENA_FAST_DECAY = 0.3
HYENA_SLOW_DECAY = 1.5
HYENA_TARGET = 1e-2
LN_EPS = 1e-5
GN_EPS = 1e-5
LANES = 128
SUBLANES = 8
PACKED_ROWS = 16
ACT = jnp.bfloat16
MLSTM_NB = 2
RWKV_NB = 4
RET_NB = 2
VMEM_LIMIT = 56 * 1024 * 1024

NN = (((1,), (0,)), ((), ()))
NT = (((1,), (1,)), ((), ()))
TN = (((0,), (0,)), ((), ()))


def _dg(a, b, dims=NN):
    return lax.dot_general(a, b, dims, preferred_element_type=F32)


def _bf(x):
    return x.astype(BF16)


def _split2(x):
    hi = _bf(x)
    return hi, _bf(x - hi.astype(F32))


def _split3(x):
    hi = _bf(x)
    r = x - hi.astype(F32)
    mid = _bf(r)
    return hi, mid, _bf(r - mid.astype(F32))


def _mm(a, b, dims=NN, passes=1):
    if passes == 1:
        return _dg(_bf(a), _bf(b), dims)
    ah, al = _split2(a)
    bh, bl = _split2(b)
    return _dg(ah, bh, dims) + (_dg(al, bh, dims) + _dg(ah, bl, dims))


def _mm_xl(a_exact, b, dims=NN):
    b1, b2, b3 = _split3(b)
    return _dg(a_exact, b1, dims) + (_dg(a_exact, b2, dims) + _dg(a_exact, b3, dims))


def _mm_xr(a, b_exact, dims=NN):
    a1, a2 = _split2(a)
    return _dg(a1, b_exact, dims) + _dg(a2, b_exact, dims)


def _iota(shape, dim):
    return lax.broadcasted_iota(jnp.int32, shape, dim)


def _tri_mask(t, rev, strict=False):
    r = _iota((t, t), 0)
    c = _iota((t, t), 1)
    if strict:
        return (c > r) if rev else (c < r)
    return (c >= r) if rev else (c <= r)


def _head_ones():
    r = _iota((GROUP_W, GROUP_W), 0)
    c = _iota((GROUP_W, GROUP_W), 1)
    return jnp.where((r >> 6) == (c >> 6), 1.0, 0.0).astype(BF16)


def _head_sum(x, ones_bd):
    return _dg(_bf(x), ones_bd)


def _cumsum_rows(x, rev):
    t = x.shape[0]
    rowi = _iota(x.shape, 0)
    sft = 1
    while sft < t:
        if rev:
            prev = jnp.where(rowi < t - sft, pltpu.roll(x, t - sft, 0), 0.0)
        else:
            prev = jnp.where(rowi >= sft, pltpu.roll(x, sft, 0), 0.0)
        x = x + prev
        sft *= 2
    return x


def _head_norm(y, ones_bd, g, b):
    mu = _head_sum(y, ones_bd) * (1.0 / HEAD_DIM)
    yc = y - mu
    var = _head_sum(yc * yc, ones_bd) * (1.0 / HEAD_DIM)
    return yc * lax.rsqrt(var + GN_EPS) * g + b


def _ln_rows(x):
    mu = jnp.mean(x, -1, keepdims=True)
    xc = x - mu
    var = jnp.mean(xc * xc, -1, keepdims=True)
    return xc * lax.rsqrt(var + LN_EPS)


def _sigmoid(x):
    return 1.0 / (1.0 + jnp.exp(-x))


def _log_sigmoid(x):
    return jnp.minimum(x, 0.0) - jnp.log(1.0 + jnp.exp(-jnp.abs(x)))


def _with_neighbors(ref, s, t, ls, c0, c1, bi=0):
    grp = PACKED_ROWS
    cur = ref[bi, pl.ds(s, t), c0:c1].astype(F32)
    p0 = pl.multiple_of(jnp.maximum(s - grp, 0), grp)
    n0 = pl.multiple_of(jnp.minimum(s + t, ls - grp), grp)
    prow = ref[bi, pl.ds(p0, grp), c0:c1].astype(F32)[grp - 1:grp]
    nrow = ref[bi, pl.ds(n0, grp), c0:c1].astype(F32)[0:1]
    prow = jnp.where(s > 0, prow, 0.0)
    nrow = jnp.where(s + t < ls, nrow, 0.0)
    rows = _iota((t, 1), 0)
    prev = jnp.where(rows == 0, prow, pltpu.roll(cur, 1, 0))
    nxt = jnp.where(rows == t - 1, nrow, pltpu.roll(cur, t - 1, 0))
    return cur, prev, nxt


def _dwconv_rows(ref, s, t, ls, c0, c1, w_ref, b_ref, bi=0):
    cur, prev, nxt = _with_neighbors(ref, s, t, ls, c0, c1, bi)
    return prev * w_ref[0:1, c0:c1] + cur * w_ref[1:2, c0:c1] + nxt * w_ref[2:3, c0:c1] + b_ref[0:1, c0:c1]


def _cparams(sem):
    return pltpu.CompilerParams(dimension_semantics=sem, vmem_limit_bytes=VMEM_LIMIT)


def _const_spec(shape):
    nd = len(shape)
    return pl.BlockSpec(shape, lambda *_: (0,) * nd, pipeline_mode=pl.Buffered(1))


def _seq_spec(ls, w, nbat=1, single=False):
    if single:
        return pl.BlockSpec((nbat, ls, w), lambda b: (b, 0, 0), pipeline_mode=pl.Buffered(1))
    return pl.BlockSpec((nbat, ls, w), lambda b: (b, 0, 0))


def _mix_nb(bsz, want):
    return want if bsz % want == 0 else 1


def _ada_kernel(c_ref, w_ref, b_ref, o_ref):
    c = c_ref[...]
    o_ref[...] = _mm(c * _sigmoid(c), w_ref[...], passes=3) + b_ref[...]


def _ada_call(cc, w, b):
    m, d = cc.shape
    n = w.shape[1]
    tn = 512
    return pl.pallas_call(
        _ada_kernel,
        grid=(n // tn,),
        in_specs=[pl.BlockSpec((m, d), lambda j: (0, 0)),
                  pl.BlockSpec((d, tn), lambda j: (0, j)),
                  pl.BlockSpec((1, tn), lambda j: (0, j))],
        out_specs=pl.BlockSpec((m, tn), lambda j: (0, j)),
        out_shape=jax.ShapeDtypeStruct((m, n), F32),
        compiler_params=_cparams(("arbitrary",)),
        name="ada_mod",
    )(cc, w, b.reshape(1, n))


SEG_W = (1152, 768, 1024, 896)


def _inproj_kernel(x_ref, sh_ref, sc_ref, w_ref, o1, o2, o3, o4):
    u = _ln_rows(x_ref[0]) * (1.0 + sc_ref[0]) + sh_ref[0]
    ub = _bf(u)
    off = 0
    for o in (o1, o2, o3, o4):
        n = o.shape[-1]
        o[0] = _dg(ub, w_ref[:, off:off + n]).astype(o.dtype)
        off += n


def _mod_spec(arr):
    d = arr.shape[-1]
    if arr.shape[0] == 1:
        return pl.BlockSpec((1, 1, d), lambda b, i: (0, 0, 0))
    return pl.BlockSpec((1, 1, d), lambda b, i: (b, 0, 0))


def _inproj_call(x, shift, scale, w_bf):
    bsz, ls, d = x.shape
    tm = min(512, ls)
    return pl.pallas_call(
        _inproj_kernel,
        grid=(bsz, ls // tm),
        in_specs=[pl.BlockSpec((1, tm, d), lambda b, i: (b, i, 0)),
                  _mod_spec(shift), _mod_spec(scale),
                  _const_spec(w_bf.shape)],
        out_specs=[pl.BlockSpec((1, tm, n), lambda b, i: (b, i, 0)) for n in SEG_W],
        out_shape=[jax.ShapeDtypeStruct((bsz, ls, n), ACT) for n in SEG_W],
        compiler_params=_cparams(("parallel", "parallel")),
        name="in_proj",
    )(x, shift, scale, w_bf)


def _mlstm_solve(chunks, states):
    t = chunks[0][0].shape[0]
    hd = HEAD_DIM
    assert t == LANES
    lane = _iota((t, LANES), 1)
    rowi = _iota((t, LANES), 0)
    lo = lane < hd
    chains = []
    for di, (q, k, v, gates, rev) in enumerate(chunks):
        d = 1 if rev else 0
        mask = _tri_mask(t, rev)
        tri = jnp.where(mask, 1.0, 0.0).astype(BF16)
        bc = _mm_xl(tri, _log_sigmoid(gates))
        gates_t = gates.T
        bc_t = bc.T
        for h in range(N_HEADS_G):
            p, e = divmod(h, 2)
            ci, cf = 8 * d + h, 8 * d + 4 + h
            cols = slice(LANES * p, LANES * (p + 1))
            half = lo if e == 0 else jnp.logical_not(lo)
            b_rep = jnp.broadcast_to(bc[:, cf:cf + 1], (t, LANES))
            c_rep = jnp.broadcast_to(gates[:, ci:ci + 1], (t, LANES)) - b_rep
            chains.append(dict(
                di=di, h=h, p=p, e=e, half=half, mask=mask, rev=rev, last=0 if rev else t - 1,
                q=q[:, cols], k=k[:, cols], v=v[:, cols], b_rep=b_rep, c_rep=c_rep,
                c_row=gates_t[ci:ci + 1, :] - bc_t[cf:cf + 1, :],
                cn=states[di][0][p], m=states[di][1][h]))
    for c in chains:
        c["qm"] = jnp.where(c["half"], c["q"], 0.0)
        c["s_raw"] = _dg(_bf(c["qm"]), _bf(c["k"]), NT)
    for c in chains:
        cm = c["c_rep"]
        sft = 1
        while sft < t:
            if c["rev"]:
                prev = jnp.where(rowi < t - sft, pltpu.roll(cm, t - sft, 0), -jnp.inf)
            else:
                prev = jnp.where(rowi >= sft, pltpu.roll(cm, sft, 0), -jnp.inf)
            cm = jnp.maximum(cm, prev)
            sft *= 2
        c["big_m"] = jnp.maximum(cm, c["m"])
        c["s"] = c["s_raw"] * jnp.exp(jnp.where(c["mask"], c["c_row"] - c["big_m"], -jnp.inf))
        c["w_inter"] = jnp.exp(c["m"] - c["big_m"])
    for c in chains:
        c["v_aug"] = _bf(jnp.where(c["half"], c["v"], 1.0))
        lhs = jnp.concatenate([_bf(c["s"]), _bf(c["qm"] * c["w_inter"])], axis=1)
        rhs = jnp.concatenate([c["v_aug"], _bf(c["cn"])], axis=0)
        c["res"] = _dg(lhs, rhs)
    for c in chains:
        den = pltpu.roll(c["res"], hd, 1)
        c["hh"] = c["res"] / jnp.maximum(jnp.abs(den), jnp.exp(-(c["b_rep"] + c["big_m"])))
    for c in chains:
        last = c["last"]
        m_last = c["big_m"][last:last + 1, :]
        c["m_new"] = c["b_rep"][last:last + 1, :] + m_last
        c["kw"] = _bf(jnp.where(c["half"], c["k"], 0.0) * jnp.exp(c["c_rep"] - m_last))
        c["decay"] = jnp.exp(c["m"] - m_last)
    out = []
    row = _iota((LANES, LANES), 0)
    for di in range(len(chunks)):
        hs, cns = [], []
        for p in range(N_HEADS_G // 2):
            ce, co = [c for c in chains if c["di"] == di and c["p"] == p]
            hs.append(jnp.where(lo, ce["hh"], co["hh"]))
            upd = _dg(jnp.concatenate([ce["kw"], co["kw"]], axis=0),
                      jnp.concatenate([ce["v_aug"], co["v_aug"]], axis=0), TN)
            cns.append(jnp.where(row < hd, ce["decay"], co["decay"]) * ce["cn"] + upd)
        ms = tuple(c["m_new"] for c in chains if c["di"] == di)
        out.append((jnp.concatenate(hs, axis=-1), (tuple(cns), ms)))
    return out


def _mlstm_kernel(p_ref, pc_ref, cw_ref, cb_ref, gb_ref, ng_ref, nb_ref, o_ref, oc_ref,
                  qk_ref, qkc_ref, hf_ref, hb_ref, hfc_ref, hbc_ref):
    t = CHUNK
    nbat = p_ref.shape[0]
    ones_bd = _head_ones()

    def conv_pass(src, dst):
        ls = src.shape[1]

        def body(c, carry):
            s = pl.multiple_of(c * t, t)
            for bi in range(nbat):
                y = _dwconv_rows(src, s, t, ls, 0, 2 * GROUP_W, cw_ref, cb_ref, bi)
                dst[bi, pl.ds(s, t), :] = (y * _sigmoid(y)).astype(dst.dtype)
            return carry

        lax.fori_loop(0, ls // t, body, 0)

    conv_pass(p_ref, qk_ref)
    conv_pass(pc_ref, qkc_ref)

    def scan(src, qk, hf, hb, states):
        ls = src.shape[1]
        n = ls // t

        def body(i, carry):
            starts = (pl.multiple_of(i * t, t), pl.multiple_of((n - 1 - i) * t, t))
            chunks = []
            for bi in range(nbat):
                for s, rev in zip(starts, (False, True)):
                    q = qk[bi, pl.ds(s, t), 0:GROUP_W].astype(F32) * (HEAD_DIM ** -0.5)
                    k = qk[bi, pl.ds(s, t), GROUP_W:2 * GROUP_W].astype(F32)
                    v = src[bi, pl.ds(s, t), 2 * GROUP_W:3 * GROUP_W].astype(F32)
                    gates = src[bi, pl.ds(s, t), 4 * GROUP_W:4 * GROUP_W + LANES].astype(F32) + gb_ref[...]
                    chunks.append((q, k, v, gates, rev))
            res = _mlstm_solve(chunks, carry)
            for bi in range(nbat):
                hf[bi, pl.ds(starts[0], t), :] = res[2 * bi][0]
                hb[bi, pl.ds(starts[1], t), :] = res[2 * bi + 1][0]
            return tuple(r[1] for r in res)

        return lax.fori_loop(0, n, body, states)

    zero = (tuple(jnp.zeros((LANES, LANES), F32) for _ in range(N_HEADS_G // 2)),
            tuple(jnp.zeros((1, LANES), F32) for _ in range(N_HEADS_G)))
    states = scan(pc_ref, qkc_ref, hfc_ref, hbc_ref, (zero,) * (2 * nbat))
    scan(p_ref, qk_ref, hf_ref, hb_ref, states)

    def finish(src, hf, hb, out):
        ls = src.shape[1]

        def body(c, carry):
            s = pl.multiple_of(c * t, t)
            for bi in range(nbat):
                h = hf[bi, pl.ds(s, t), :] + hb[bi, pl.ds(s, t), :]
                o = src[bi, pl.ds(s, t), 3 * GROUP_W:4 * GROUP_W].astype(F32)
                y = _sigmoid(o) * _head_norm(h, ones_bd, ng_ref[...], nb_ref[...])
                out[bi, pl.ds(s, t), :] = y.astype(out.dtype)
            return carry

        lax.fori_loop(0, ls // t, body, 0)

    finish(p_ref, hf_ref, hb_ref, o_ref)
    finish(pc_ref, hfc_ref, hbc_ref, oc_ref)


def _mixer_out(bsz, ls, lc, nbat=1):
    return ([_seq_spec(ls, GROUP_W, nbat), _seq_spec(lc, GROUP_W, nbat)],
            [jax.ShapeDtypeStruct((bsz, ls, GROUP_W), ACT), jax.ShapeDtypeStruct((bsz, lc, GROUP_W), ACT)])


def _mlstm_call(p, pc, conv_w, conv_b, gate_b, ng, nb):
    bsz, ls, w = p.shape
    lc = pc.shape[1]
    nbat = _mix_nb(bsz, MLSTM_NB)
    gb = jnp.zeros((1, LANES), F32).at[0, :4 * N_HEADS_G].set(gate_b.reshape(-1))
    consts = [conv_w, conv_b.reshape(1, -1), gb, ng.reshape(1, -1), nb.reshape(1, -1)]
    out_specs, out_shape = _mixer_out(bsz, ls, lc, nbat)
    return pl.pallas_call(
        _mlstm_kernel,
        grid=(bsz // nbat,),
        in_specs=[_seq_spec(ls, w, nbat), _seq_spec(lc, w, nbat)] + [_const_spec(a.shape) for a in consts],
        out_specs=out_specs,
        out_shape=out_shape,
        scratch_shapes=[pltpu.VMEM((nbat, ls, 2 * GROUP_W), BF16), pltpu.VMEM((nbat, lc, 2 * GROUP_W), BF16),
                        pltpu.VMEM((nbat, ls, GROUP_W), F32), pltpu.VMEM((nbat, ls, GROUP_W), F32),
                        pltpu.VMEM((nbat, lc, GROUP_W), F32), pltpu.VMEM((nbat, lc, GROUP_W), F32)],
        compiler_params=_cparams(("parallel",)),
        name="mlstm_mixer",
    )(p, pc, *consts)


def _ret_log_gamma(h, rev):
    hh = (N_HEADS_G - 1 - h) if rev else h
    return math.log(1.0 - 2.0 ** (-5.0 - hh))


def _ret_solve(chunks, states):
    t = chunks[0][0].shape[0]
    hd = HEAD_DIM
    lo = _iota((t, LANES), 1) < hd
    r = _iota((t, t), 0)
    cc = _iota((t, t), 1)
    work = []
    for di, (q, k, v, rev) in enumerate(chunks):
        diff = ((cc - r) if rev else (r - cc)).astype(F32)
        valid = diff >= 0.0
        ordinal = _iota((t, 1), 0).astype(F32)
        if rev:
            ordinal = (t - 1.0) - ordinal
        for p in range(N_HEADS_G // 2):
            cols = slice(LANES * p, LANES * (p + 1))
            lg = [_ret_log_gamma(2 * p + e, rev) for e in range(2)]
            work.append(dict(di=di, p=p, q=q[:, cols], k=k[:, cols], v=v[:, cols], lg=lg, diff=diff, valid=valid,
                             ordinal=ordinal, s0=states[di][p]))
    for w in work:
        kb = _bf(w["k"])
        w["qm"] = [jnp.where(lo if e == 0 else jnp.logical_not(lo), w["q"], 0.0) for e in range(2)]
        w["s_raw"] = [_dg(_bf(w["qm"][e]), kb, NT) for e in range(2)]
    for w in work:
        dm = [jnp.where(w["valid"], jnp.exp(jnp.where(w["valid"], w["diff"], 0.0) * w["lg"][e]), 0.0) for e in range(2)]
        q_dec = jnp.where(lo, jnp.exp((w["ordinal"] + 1.0) * w["lg"][0]), jnp.exp((w["ordinal"] + 1.0) * w["lg"][1]))
        k_dec = jnp.where(lo, jnp.exp((t - 1.0 - w["ordinal"]) * w["lg"][0]),
                          jnp.exp((t - 1.0 - w["ordinal"]) * w["lg"][1]))
        vb = _bf(w["v"])
        zero = jnp.zeros_like(vb)
        vm = [jnp.where(lo, vb, zero), jnp.where(lo, zero, vb)]
        lhs = jnp.concatenate([_bf(w["s_raw"][0] * dm[0]), _bf(w["s_raw"][1] * dm[1]), _bf(w["q"] * q_dec)], axis=1)
        rhs = jnp.concatenate([vm[0], vm[1], _bf(w["s0"])], axis=0)
        w["o"] = _dg(lhs, rhs)
        kd = _bf(w["k"] * k_dec)
        kz = jnp.zeros_like(kd)
        upd = _dg(jnp.concatenate([jnp.where(lo, kd, kz), jnp.where(lo, kz, kd)], axis=0),
                  jnp.concatenate(vm, axis=0), TN)
        rows = _iota((LANES, 1), 0)
        cd = jnp.where(rows < hd, math.exp(t * w["lg"][0]), math.exp(t * w["lg"][1]))
        w["s_new"] = w["s0"] * cd + upd
    out = []
    for di in range(len(chunks)):
        mine = [w for w in work if w["di"] == di]
        out.append((jnp.concatenate([w["o"] for w in mine], axis=-1), tuple(w["s_new"] for w in mine)))
    return out


def _rope(x, cos_t, sin_t):
    w = x.shape[-1]
    lane = _iota(x.shape, 1)
    partner = jnp.where((lane & 31) < ROPE_PAIRS, pltpu.roll(x, w - ROPE_PAIRS, 1), pltpu.roll(x, ROPE_PAIRS, 1))
    return x * cos_t + partner * sin_t


def _ret_kernel(p_ref, pc_ref, cos_ref, sin_ref, ng_ref, nb_ref, o_ref, oc_ref,
                hf_ref, hb_ref, hfc_ref, hbc_ref):
    t = CHUNK
    nbat = p_ref.shape[0]
    ones_bd = _head_ones()

    def scan(src, hf, hb, states, rotate):
        ls = src.shape[1]
        n = ls // t

        def body(i, carry):
            starts = (pl.multiple_of(i * t, t), pl.multiple_of((n - 1 - i) * t, t))
            chunks = []
            for bi in range(nbat):
                for s, rev in zip(starts, (False, True)):
                    q = src[bi, pl.ds(s, t), 0:GROUP_W].astype(F32) * (HEAD_DIM ** -0.5)
                    k = src[bi, pl.ds(s, t), GROUP_W:2 * GROUP_W].astype(F32)
                    v = src[bi, pl.ds(s, t), 2 * GROUP_W:3 * GROUP_W].astype(F32)
                    if rotate:
                        cos_t = cos_ref[pl.ds(s, t), :]
                        sin_t = sin_ref[pl.ds(s, t), :]
                        q = _rope(q, cos_t, sin_t)
                        k = _rope(k, cos_t, sin_t)
                    chunks.append((q, k, v, rev))
            res = _ret_solve(chunks, carry)
            for bi in range(nbat):
                hf[bi, pl.ds(starts[0], t), :] = res[2 * bi][0]
                hb[bi, pl.ds(starts[1], t), :] = res[2 * bi + 1][0]
            return tuple(r[1] for r in res)

        return lax.fori_loop(0, n, body, states)

    zero = tuple(jnp.zeros((LANES, LANES), F32) for _ in range(N_HEADS_G // 2))
    states = scan(pc_ref, hfc_ref, hbc_ref, (zero,) * (2 * nbat), False)
    scan(p_ref, hf_ref, hb_ref, states, True)

    def finish(src, hf, hb, out):
        ls = src.shape[1]

        def body(c, carry):
            s = pl.multiple_of(c * t, t)
            for bi in range(nbat):
                h = hf[bi, pl.ds(s, t), :] + hb[bi, pl.ds(s, t), :]
                g = src[bi, pl.ds(s, t), 3 * GROUP_W:4 * GROUP_W].astype(F32)
                y = g * _sigmoid(g) * _head_norm(h, ones_bd, ng_ref[...], nb_ref[...])
                out[bi, pl.ds(s, t), :] = y.astype(out.dtype)
            return carry

        lax.fori_loop(0, ls // t, body, 0)

    finish(p_ref, hf_ref, hb_ref, o_ref)
    finish(pc_ref, hfc_ref, hbc_ref, oc_ref)


def _rope_tables(ls):
    pos = np.arange(ls)
    freqs = ROPE_BASE ** (-np.arange(ROPE_PAIRS, dtype=np.float64) / ROPE_PAIRS)
    row = (pos // GRID_W).astype(np.float64)[:, None] * freqs
    col = (pos % GRID_W).astype(np.float64)[:, None] * freqs
    ang = np.concatenate([row, row, col, col], -1)
    sign = np.tile(np.concatenate([-np.ones(ROPE_PAIRS), np.ones(ROPE_PAIRS)]), 2)
    cos_t = np.tile(np.cos(ang), (1, N_HEADS_G)).astype(np.float32)
    sin_t = np.tile(np.sin(ang) * sign, (1, N_HEADS_G)).astype(np.float32)
    return cos_t, sin_t


def _ret_call(p, pc, ng, nb):
    bsz, ls, w = p.shape
    lc = pc.shape[1]
    nbat = _mix_nb(bsz, RET_NB)
    cos_t, sin_t = _rope_tables(ls)
    consts = [cos_t, sin_t, ng.reshape(1, -1), nb.reshape(1, -1)]
    out_specs, out_shape = _mixer_out(bsz, ls, lc, nbat)
    return pl.pallas_call(
        _ret_kernel,
        grid=(bsz // nbat,),
        in_specs=[_seq_spec(ls, w, nbat), _seq_spec(lc, w, nbat)] + [_const_spec(a.shape) for a in consts],
        out_specs=out_specs,
        out_shape=out_shape,
        scratch_shapes=[pltpu.VMEM((nbat, ls, GROUP_W), F32), pltpu.VMEM((nbat, ls, GROUP_W), F32),
                        pltpu.VMEM((nbat, lc, GROUP_W), F32), pltpu.VMEM((nbat, lc, GROUP_W), F32)],
        compiler_params=_cparams(("parallel",)),
        name="retention_mixer",
    )(p, pc, *consts)


RW_INV_PASSES = 1


def _rwkv_shifted(src, s, t, mu_ref, bi=0):
    ls = src.shape[1]
    w = src.shape[2]
    cur, prev, nxt = _with_neighbors(src, s, t, ls, 0, w, bi)
    return cur + mu_ref[0:1, :] * (prev - cur) + mu_ref[1:2, :] * (nxt - cur)


def _rwkv_prep(z, prm, rev, ones_bd):
    w0_ref, w2_ref, a0_ref, a2_ref, kks_ref, ka_ref = prm
    t = z.shape[0]
    d = 1 if rev else 0
    g = GROUP_W
    r, k, v = z[:, 0:g], z[:, g:2 * g], z[:, 2 * g:3 * g]
    wlo = z[:, 3 * g:3 * g + 32]
    alo = z[:, 3 * g + 32:3 * g + 64]
    kk = k * kks_ref[...]
    kk = kk * lax.rsqrt(_head_sum(kk * kk, ones_bd) + 1e-12)
    w_pre = w0_ref[d:d + 1, :] + _mm(jnp.tanh(wlo), w2_ref[d])
    sp = jnp.maximum(-w_pre, 0.0) + jnp.log(1.0 + jnp.exp(-jnp.abs(w_pre)))
    logw = -jnp.exp(-sp - 0.5)
    ag = _sigmoid(a0_ref[d:d + 1, :] + _mm(alo, a2_ref[d]))
    kd = k * (1.0 + (ag - 1.0) * ka_ref[...])
    a = -kk
    b = kk * ag
    cum = _cumsum_rows(logw, rev)
    e_pos = jnp.exp(cum)
    e_neg = jnp.exp(-cum)
    at = a * jnp.exp(cum - logw)
    rt = r * e_pos
    bt = b * e_neg
    kt = kd * e_neg
    last = 0 if rev else t - 1
    return dict(at=_bf(at), rt=_bf(rt), bt=_bf(bt), kt=_bf(kt), v=_bf(v), w_last=e_pos[last:last + 1, :], rev=rev)


def _rwkv_solve(preps, states):
    t = preps[0]["v"].shape[0]
    hd = HEAD_DIM
    t2 = 2 * t
    n_lvl = int(math.log2(t))
    lo = _iota((t, LANES), 1) < hd
    rowi = _iota((t2, 2 * t2), 0) & (t - 1)
    colj = _iota((t2, 2 * t2), 1) & (t - 1)

    def expand(x):
        zero = jnp.zeros_like(x)
        return jnp.concatenate([jnp.where(lo, x, zero), jnp.where(lo, zero, x)], axis=0)

    work = []
    for di, pr in enumerate(preps):
        strict2 = (colj > rowi) if pr["rev"] else (colj < rowi)
        incl2 = (colj >= rowi) if pr["rev"] else (colj <= rowi)
        for p in range(N_HEADS_G // 2):
            cols = slice(LANES * p, LANES * (p + 1))
            ex = {name: expand(pr[name][:, cols]) for name in ("at", "rt", "bt", "kt", "v")}
            work.append(dict(di=di, p=p, ar=jnp.concatenate([ex["at"], ex["rt"]], axis=0),
                             bk=jnp.concatenate([ex["bt"], ex["kt"]], axis=0), v=ex["v"],
                             s0=states[di][p], w_last=pr["w_last"][:, cols], strict2=strict2, incl2=incl2))
    for w in work:
        rhs = jnp.concatenate([w["bk"], _bf(w["s0"])], axis=0)
        w["gh"] = _dg(w["ar"], rhs, NT)
    for w in work:
        gh = w["gh"]
        w["lo"] = jnp.where(w["strict2"], gh[:t2, :2 * t2], 0.0)
        w["m"] = _bf(jnp.where(w["incl2"], gh[t2:, :2 * t2], 0.0))
    for w in work:
        w["u0"] = w["gh"][:t2, 2 * t2:] + _dg(_bf(w["lo"][:, t2:]), w["v"])
    diff = _iota((t2, t2), 0) ^ _iota((t2, t2), 1)
    eye = jnp.where(diff == 0, 1.0, 0.0)
    for w in work:
        w["l"] = w["lo"][:, :t2]
        w["d"] = eye + jnp.where(diff == 1, w["l"], 0.0)
    for k in range(1, n_lvl):
        join = (diff >> k) == 1
        for w in work:
            w["e"] = _mm(jnp.where(join, w["l"], 0.0), w["d"], passes=RW_INV_PASSES)
        for w in work:
            w["d"] = w["d"] + _mm(w["d"], w["e"], passes=RW_INV_PASSES)
    for w in work:
        w["uv"] = jnp.concatenate([_bf(_mm(w["d"], w["u0"], passes=RW_INV_PASSES)), w["v"]], axis=0)
    for w in work:
        y_bd = w["gh"][t2:, 2 * t2:] + _dg(w["m"], w["uv"])
        w["y"] = y_bd[:t] + y_bd[t:]
    for w in work:
        w["s_new"] = (w["s0"] + _dg(w["uv"], w["bk"], TN)) * w["w_last"]
    out = []
    for di in range(len(preps)):
        mine = [w for w in work if w["di"] == di]
        out.append((jnp.concatenate([w["y"] for w in mine], axis=-1), tuple(w["s_new"] for w in mine)))
    return out


def _rwkv_kernel(p_ref, pc_ref, mu_ref, w0_ref, w2_ref, a0_ref, a2_ref, g2_ref, kks_ref, ka_ref, rk_ref,
                 ng_ref, nb_ref, o_ref, oc_ref, hf_ref, hb_ref, hfc_ref, hbc_ref):
    t = RWKV_CHUNK
    nbat = p_ref.shape[0]
    ones_bd = _head_ones()
    prm = (w0_ref, w2_ref, a0_ref, a2_ref, kks_ref, ka_ref)

    def scan(src, hf, hb, states):
        ls = src.shape[1]
        n = ls // t

        def body(i, carry):
            starts = (pl.multiple_of(i * t, t), pl.multiple_of((n - 1 - i) * t, t))
            preps = [_rwkv_prep(_rwkv_shifted(src, s, t, mu_ref, bi), prm, rev, ones_bd)
                     for bi in range(nbat) for s, rev in zip(starts, (False, True))]
            res = _rwkv_solve(preps, carry)
            for bi in range(nbat):
                hf[bi, pl.ds(starts[0], t), :] = res[2 * bi][0].astype(hf.dtype)
                hb[bi, pl.ds(starts[1], t), :] = res[2 * bi + 1][0].astype(hb.dtype)
            return tuple(r[1] for r in res)

        return lax.fori_loop(0, n, body, states)

    zero = tuple(jnp.zeros((LANES, LANES), F32) for _ in range(N_HEADS_G // 2))
    states = scan(pc_ref, hfc_ref, hbc_ref, (zero,) * (2 * nbat))
    scan(p_ref, hf_ref, hb_ref, states)

    def finish(src, hf, hb, out):
        ls = src.shape[1]
        g = GROUP_W

        def body(c, carry):
            s = pl.multiple_of(c * t, t)
            for bi in range(nbat):
                z = _rwkv_shifted(src, s, t, mu_ref, bi)
                r, k, v = z[:, 0:g], z[:, g:2 * g], z[:, 2 * g:3 * g]
                glo = z[:, 3 * g + 64:3 * g + 128]
                gate = _mm(_sigmoid(glo), g2_ref[...])
                bonus = _head_sum(r * k * rk_ref[...], ones_bd) * v
                h = hf[bi, pl.ds(s, t), :].astype(F32) + hb[bi, pl.ds(s, t), :].astype(F32)
                y = gate *(_head_norm(h, ones_bd, ng_ref[...], nb_ref[...]) + bonus)
                out[bi, pl.ds(s, t), :] = y.astype(out.dtype)
            return carry

        lax.fori_loop(0, ls // t, body, 0)

    finish(p_ref, hf_ref, hb_ref, o_ref)
    finish(pc_ref, hfc_ref, hbc_ref, oc_ref)


def _rwkv_call(p, pc, mu, w0, w2, a0, a2, g2, kk, ka, rk, ng, nb):
    bsz, ls, w = p.shape
    lc = pc.shape[1]
    nbat = _mix_nb(bsz, RWKV_NB)
    consts = [mu, w0, w2, a0, a2, g2, kk.reshape(1, -1), ka.reshape(1, -1), rk.reshape(1, -1),
              ng.reshape(1, -1), nb.reshape(1, -1)]
    out_specs, out_shape = _mixer_out(bsz, ls, lc, nbat)
    return pl.pallas_call(
        _rwkv_kernel,
        grid=(bsz // nbat,),
        in_specs=[_seq_spec(ls, w, nbat, single=True), _seq_spec(lc, w, nbat)] + [_const_spec(a.shape) for a in consts],
        out_specs=out_specs,
        out_shape=out_shape,
        scratch_shapes=[pltpu.VMEM((nbat, ls, GROUP_W), ACT), pltpu.VMEM((nbat, ls, GROUP_W), ACT),
                        pltpu.VMEM((nbat, lc, GROUP_W), ACT), pltpu.VMEM((nbat, lc, GROUP_W), ACT)],
        compiler_params=_cparams(("parallel",)),
        name="rwkv7_mixer",
    )(p, pc, *consts)


def _dft_mats(ls):
    n = 2 * ls
    idx = np.arange(ls, dtype=np.int64)
    prod = (idx[:, None] * idx[None, :]) % n
    ang = 2.0 * np.pi * prod.astype(np.float64) / n
    return np.cos(ang), -np.sin(ang)


def _hyena_feats(ls):
    pos = np.arange(ls, dtype=np.float64)
    tt = np.linspace(0.0, 1.0, ls)[:, None]
    ang = 2.0 * math.pi * pos[:, None] / ls
    bands = np.linspace(1e-4, HYENA_BANDS - 1, HYENA_BANDS)[None, :]
    feats = np.concatenate([tt, np.cos(bands * ang), -np.sin(bands * ang)], -1)
    feats = np.pad(feats, ((0, 0), (0, LANES - HYENA_EMB))).astype(np.float32)
    max_decay = math.log(HYENA_TARGET) / HYENA_FAST_DECAY
    min_decay = math.log(HYENA_TARGET) / HYENA_SLOW_DECAY
    deltas = np.abs(np.linspace(min_decay, max_decay, GROUP_W))
    window = np.exp(-tt * deltas).astype(np.float32)
    return feats, window


def _hyena_taps_kernel(f_ref, win_ref, w1_ref, b1_ref, w2_ref, b2_ref, w3_ref, fr_ref, sum_ref, dif_ref, nyq_ref):
    fr = fr_ref[...]
    h = jnp.sin(fr * (_mm(f_ref[...], w1_ref[...], passes=3) + b1_ref[...]))
    h = jnp.sin(fr * (_mm(h, w2_ref[...], passes=3) + b2_ref[...]))
    h = _mm(h, w3_ref[...], passes=3)
    win = win_ref[...]
    win2 = jnp.concatenate([win, win], axis=-1)
    hf = h[:, 0:2 * GROUP_W] * win2
    hb = h[:, 2 * GROUP_W:4 * GROUP_W] * win2
    sum_ref[...] = hf + hb
    dif_ref[...] = hf - hb
    ls = h.shape[0]
    alt = 1.0 - 2.0 * (_iota((ls, 1), 0) & 1).astype(F32)
    nyq_ref[...] = jnp.sum((hf + hb) * alt, axis=0, keepdims=True) * (1.0 / (2.0 * ls))


def _hyena_spec_kernel(fch_ref, fcl_ref, fsh_ref, fsl_ref, sum_ref, dif_ref, kre_ref, kim_ref):
    j = pl.program_id(0)
    tf = fch_ref.shape[0]
    ls = fch_ref.shape[1]
    hs_h, hs_l = _split2(sum_ref[...])
    hd_h, hd_l = _split2(dif_ref[...])
    kre = _dg(fch_ref[...], hs_h) + (_dg(fcl_ref[...], hs_h) + _dg(fch_ref[...], hs_l))
    kim = _dg(fsh_ref[...], hd_h) + (_dg(fsl_ref[...], hd_h) + _dg(fsh_ref[...], hd_l))
    f_idx = _iota((tf, 1), 0) + j * tf
    scale = jnp.where(f_idx == 0, 1.0, 2.0) * (1.0 / (2.0 * ls))
    kre_ref[...] = kre * scale
    kim_ref[...] = kim * scale


def _hyena_filters(ls, w1, b1, w2, b2, w3, freq, fmats):
    feats, window = _hyena_feats(ls)
    w1p = jnp.pad(w1, ((0, LANES - HYENA_EMB), (0, 0)))
    args = [jnp.asarray(feats), jnp.asarray(window), w1p, b1.reshape(1, -1), w2, b2.reshape(1, -1), w3,
            freq.reshape(1, -1)]
    hsum, hdif, knyq = pl.pallas_call(
        _hyena_taps_kernel,
        out_shape=[jax.ShapeDtypeStruct((ls, 2 * GROUP_W), F32)] * 2 + [jax.ShapeDtypeStruct((1, 2 * GROUP_W), F32)],
        compiler_params=pltpu.CompilerParams(vmem_limit_bytes=VMEM_LIMIT),
        name="hyena_taps",
    )(*args)
    fch, fcl, fsh, fsl = fmats
    tf = min(512, ls)
    fspec = pl.BlockSpec((tf, ls), lambda j: (j, 0))
    tspec = pl.BlockSpec((ls, 2 * GROUP_W), lambda j: (0, 0))
    ospec = pl.BlockSpec((tf, 2 * GROUP_W), lambda j: (j, 0))
    kre, kim = pl.pallas_call(
        _hyena_spec_kernel,
        grid=(ls // tf,),
        in_specs=[fspec, fspec, fspec, fspec, tspec, tspec],
        out_specs=[ospec, ospec],
        out_shape=[jax.ShapeDtypeStruct((ls, 2 * GROUP_W), F32)] * 2,
        compiler_params=_cparams(("arbitrary",)),
        name="hyena_spectra",
    )(fch, fcl, fsh, fsl, hsum, hdif)
    return kre, kim, knyq


def _hyena_kernel(p_ref, fc_ref, fs_ref, kre_ref, kim_ref, knyq_ref, cw_ref, cb_ref, d_ref, ng_ref, nb_ref,
                  o_ref, ub_ref, yre_ref, yim_ref, z_ref, nyq_ref):
    ls = p_ref.shape[1]
    tf = min(512, ls)
    nt = ls // tf
    g = GROUP_W
    ones_bd = _head_ones()

    def alt_sign(s):
        rows = _iota((tf, 1), 0) + s
        return 1.0 - 2.0 * (rows & 1).astype(F32)

    def long_conv(order):
        c0 = order * g

        def fwd(j, carry):
            s = pl.multiple_of(j * tf, tf)
            ure = _dg(fc_ref[pl.ds(s, tf), :], ub_ref[...])
            uim = _dg(fs_ref[pl.ds(s, tf), :], ub_ref[...])
            kre = kre_ref[pl.ds(s, tf), c0:c0 + g]
            kim = kim_ref[pl.ds(s, tf), c0:c0 + g]
            yre_ref[pl.ds(s, tf), :] = _bf(ure * kre - uim * kim)
            yim_ref[pl.ds(s, tf), :] = _bf(ure * kim + uim * kre)
            return carry

        lax.fori_loop(0, nt, fwd, 0)

    def inv_tile(s, order):
        c0 = order * g
        y = _dg(fc_ref[pl.ds(s, tf), :], yre_ref[...]) + _dg(fs_ref[pl.ds(s, tf), :], yim_ref[...])
        return y + alt_sign(s) * (nyq_ref[...] * knyq_ref[0:1, c0:c0 + g])

    def prep(j, acc):
        s = pl.multiple_of(j * tf, tf)
        v = _dwconv_rows(p_ref, s, tf, ls, 0, g, cw_ref, cb_ref)
        z_ref[pl.ds(s, tf), :] = v
        ub_ref[pl.ds(s, tf), :] = _bf(v)
        return acc + jnp.sum(v * alt_sign(s), axis=0, keepdims=True)

    nyq_ref[...] = lax.fori_loop(0, nt, prep, jnp.zeros((1, g), F32))
    long_conv(0)

    def mid(j, acc):
        s = pl.multiple_of(j * tf, tf)
        v = z_ref[pl.ds(s, tf), :]
        x1 = _dwconv_rows(p_ref, s, tf, ls, g, 2 * g, cw_ref, cb_ref)
        z2 = x1 * (inv_tile(s, 0) + v * d_ref[0:1, :])
        z_ref[pl.ds(s, tf), :] = z2
        ub_ref[pl.ds(s, tf), :] = _bf(z2)
        return acc + jnp.sum(z2 * alt_sign(s), axis=0, keepdims=True)

    nyq2 = lax.fori_loop(0, nt, mid, jnp.zeros((1, g), F32))
    nyq_ref[...] = nyq2
    long_conv(1)

    def fin(j, carry):
        s = pl.multiple_of(j * tf, tf)
        z2 = z_ref[pl.ds(s, tf), :]
        x2 = _dwconv_rows(p_ref, s, tf, ls, 2 * g, 3 * g, cw_ref, cb_ref)
        y = x2 * (inv_tile(s, 1) + z2 * d_ref[1:2, :])
        o_ref[0, pl.ds(s, tf), :] = _head_norm(y, ones_bd, ng_ref[...], nb_ref[...]).astype(o_ref.dtype)
        return carry

    lax.fori_loop(0, nt, fin, 0)


def _hyena_call(p, fc, fs, kre, kim, knyq, conv_w, conv_b, d_bias, ng, nb):
    bsz, ls, w = p.shape
    consts = [fc, fs, kre, kim, knyq, conv_w, conv_b.reshape(1, -1), d_bias, ng.reshape(1, -1), nb.reshape(1, -1)]
    return pl.pallas_call(
        _hyena_kernel,
        grid=(bsz,),
        in_specs=[_seq_spec(ls, w)] + [_const_spec(a.shape) for a in consts],
        out_specs=_seq_spec(ls, GROUP_W),
        out_shape=jax.ShapeDtypeStruct((bsz, ls, GROUP_W), ACT),
        scratch_shapes=[pltpu.VMEM((ls, GROUP_W), BF16), pltpu.VMEM((ls, GROUP_W), BF16),
                        pltpu.VMEM((ls, GROUP_W), BF16), pltpu.VMEM((ls, GROUP_W), F32),
                        pltpu.VMEM((1, GROUP_W), F32)],
        compiler_params=_cparams(("parallel",)),
        name="hyena_mixer",
    )(p, *consts)


def _outproj_kernel(ya, yb, yr, yw, x_ref, gate_ref, w_ref, g_ref, b_ref, o_ref, *, alpha):
    y = None
    for i, part in enumerate((ya, yb, yr, yw)):
        c = _dg(_bf(part[0]), w_ref[i * GROUP_W:(i + 1) * GROUP_W, :])
        y = c if y is None else y + c
    xn = alpha * x_ref[0] + gate_ref[0] * y
    o_ref[0] = _ln_rows(xn) * g_ref[...] + b_ref[...]


def _outproj_call(parts, x, gate, w_bf, g, b, alpha):
    bsz, ls, d = x.shape
    tm = min(512, ls)
    row = lambda w: pl.BlockSpec((1, tm, w), lambda bb, i: (bb, i, 0))
    consts = [w_bf, g.reshape(1, -1), b.reshape(1, -1)]
    return pl.pallas_call(
        functools.partial(_outproj_kernel, alpha=alpha),
        grid=(bsz, ls // tm),
        in_specs=[row(GROUP_W)] * 4 + [row(d), _mod_spec(gate)] + [_const_spec(a.shape) for a in consts],
        out_specs=row(d),
        out_shape=jax.ShapeDtypeStruct((bsz, ls, d), F32),
        compiler_params=_cparams(("parallel", "parallel")),
        name="out_proj_ln",
    )(*parts, x, gate, *consts)


def _ffn_kernel(xp_ref, x_ref, xn_ref, sh_ref, sc_ref, gate_ref, wa_ref, wb_ref, cw_ref, cb_ref, wd_ref,
                g_ref, b_ref, o_ref, *, alpha):
    i = pl.program_id(1)
    nt = pl.num_programs(1)
    tl = x_ref.shape[1]
    h8 = SUBLANES
    xm = x_ref[0]
    xa = jnp.concatenate([xp_ref[0], xm, xn_ref[0]], axis=0)
    u = _bf(_ln_rows(xa) * (1.0 + sc_ref[0]) + sh_ref[0])
    rows = _iota((tl + 2 * h8, 1), 0)
    valid = jnp.logical_and(jnp.logical_or(rows >= h8, i > 0), jnp.logical_or(rows < tl + h8, i < nt - 1))
    um = u[h8:tl + h8]
    nj = wa_ref.shape[0]

    def body(j, acc):
        a = jnp.where(valid, _dg(u, wa_ref[j]), 0.0)
        bb = _dg(um, wb_ref[j])
        cw = cw_ref[j]
        prev = pltpu.roll(a, 1, 0)[h8:tl + h8]
        nxt = pltpu.roll(a, tl + 2 * h8 - 1, 0)[h8:tl + h8]
        ac = prev * cw[0:1] + a[h8:tl + h8] * cw[1:2] + nxt * cw[2:3] + cb_ref[j]
        hid = ac * _sigmoid(ac) * bb
        return acc + _dg(_bf(hid), wd_ref[j])

    f = lax.fori_loop(0, nj, body, jnp.zeros(xm.shape, F32), unroll=True)
    xn = alpha * xm + gate_ref[0] * f
    o_ref[0] = _ln_rows(xn) * g_ref[...] + b_ref[...]


def _ffn_call(x, shift, scale, gate, wa, wb, cw, cb, wd, g, b, alpha):
    bsz, ls, d = x.shape
    tl = min(512, ls)
    nb8 = tl // SUBLANES
    last8 = ls // SUBLANES - 1
    consts = [wa, wb, cw, cb, wd, g.reshape(1, -1), b.reshape(1, -1)]
    return pl.pallas_call(
        functools.partial(_ffn_kernel, alpha=alpha),
        grid=(bsz, ls // tl),
        in_specs=[pl.BlockSpec((1, SUBLANES, d), lambda bb, i: (bb, jnp.maximum(i * nb8 - 1, 0), 0)),
                  pl.BlockSpec((1, tl, d), lambda bb, i: (bb, i, 0)),
                  pl.BlockSpec((1, SUBLANES, d), lambda bb, i: (bb, jnp.minimum((i + 1) * nb8, last8), 0)),
                  _mod_spec(shift), _mod_spec(scale), _mod_spec(gate)] + [_const_spec(a.shape) for a in consts],
        out_specs=pl.BlockSpec((1, tl, d), lambda bb, i: (bb, i, 0)),
        out_shape=jax.ShapeDtypeStruct((bsz, ls, d), F32),
        compiler_params=_cparams(("parallel", "parallel")),
        name="conv_ffn_ln",
    )(x, x, x, shift, scale, gate, *consts)


FFN_COLS = 256


def _prep_w_in(w):
    d = w.shape[0]
    m = 4 * GROUP_W + 4 * N_HEADS_G
    pad = jnp.zeros((d, SEG_W[0] - m), w.dtype)
    return _bf(jnp.concatenate([w[:, :m], pad, w[:, m:]], axis=1))


def _prep_ffn(w_up, conv_w, conv_b, w_down):
    d, two_ff = w_up.shape
    dff = two_ff // 2
    nj = dff // FFN_COLS
    wa = _bf(w_up[:, :dff]).reshape(d, nj, FFN_COLS).transpose(1, 0, 2)
    wb = _bf(w_up[:, dff:]).reshape(d, nj, FFN_COLS).transpose(1, 0, 2)
    cw = conv_w.reshape(3, nj, FFN_COLS).transpose(1, 0, 2)
    cb = conv_b.reshape(nj, 1, FFN_COLS)
    wd = _bf(w_down).reshape(nj, FFN_COLS, d)
    return wa, wb, cw, cb, wd


@functools.lru_cache(maxsize=None)
def _dft_inputs(ls):
    out = []
    for m in _dft_mats(ls):
        hi = m.astype(np.float32).astype(BF16)
        lo = (m - hi.astype(np.float64)).astype(np.float32).astype(BF16)
        out += [hi, lo]
    return tuple(out)


def kernel(x, c, ctx, c_ctx, ada_w, ada_b, w_in, mlstm_conv_w, mlstm_conv_b, mlstm_gate_b, hyena_conv_w, hyena_conv_b, hyena_w1, hyena_b1, hyena_w2, hyena_b2, hyena_w3, hyena_freq, hyena_d, rwkv_mu, rwkv_w0, rwkv_w2, rwkv_a0, rwkv_a2, rwkv_g2, rwkv_kk, rwkv_ka, rwkv_rk, out_norm_g, out_norm_b, w_out, ln1_g, ln1_b, ffn_w_up, ffn_conv_w, ffn_conv_b, ffn_w_down, ln2_g, ln2_b):
    bsz, ls, d = x.shape
    lc = ctx.shape[1]
    depth = ada_w.shape[0]
    alpha = float((2 * depth) ** 0.25)
    g = GROUP_W

    n_rows = -(-(bsz + 1) // SUBLANES) * SUBLANES
    cc = jnp.concatenate([c, c_ctx[None, :], jnp.zeros((n_rows - bsz - 1, d), F32)], axis=0)

    dft_l = _dft_inputs(ls)
    dft_c = _dft_inputs(lc)

    xc = ctx
    for l in range(depth):
        last = l == depth - 1
        mod_all = _ada_call(cc, ada_w[l], ada_b[l])
        mod = [mod_all[:bsz, i * d:(i + 1) * d].reshape(bsz, 1, d) for i in range(6)]
        modc = [mod_all[bsz:bsz + 1, i * d:(i + 1) * d].reshape(1, 1, d) for i in range(6)]
        w_in_bf = _prep_w_in(w_in[l])
        ng, nb = out_norm_g[l], out_norm_b[l]
        gs = lambda i, t: t[i * g:(i + 1) * g]

        pa, ph, pr, pw = _inproj_call(x, mod[0], mod[1], w_in_bf)
        pa_c, ph_c, pr_c, pw_c = _inproj_call(xc, modc[0], modc[1], w_in_bf)

        ya, ya_c = _mlstm_call(pa, pa_c, mlstm_conv_w[l], mlstm_conv_b[l], mlstm_gate_b[l], gs(0, ng), gs(0, nb))
        hy = (hyena_w1[l], hyena_b1[l], hyena_w2[l], hyena_b2[l], hyena_w3[l], hyena_freq[l])
        kre, kim, knyq = _hyena_filters(ls, *hy, dft_l)
        yb = _hyena_call(ph, dft_l[0], dft_l[2], kre, kim, knyq, hyena_conv_w[l], hyena_conv_b[l], hyena_d[l],
                         gs(1, ng), gs(1, nb))
        yr, yr_c = _ret_call(pr, pr_c, gs(2, ng), gs(2, nb))
        yw, yw_c = _rwkv_call(pw, pw_c, rwkv_mu[l], rwkv_w0[l], rwkv_w2[l], rwkv_a0[l], rwkv_a2[l], rwkv_g2[l],
                              rwkv_kk[l], rwkv_ka[l], rwkv_rk[l], gs(3, ng), gs(3, nb))

        w_out_bf = _bf(w_out[l])
        ffn_w = _prep_ffn(ffn_w_up[l], ffn_conv_w[l], ffn_conv_b[l], ffn_w_down[l])
        x = _outproj_call((ya, yb, yr, yw), x, mod[2], w_out_bf, ln1_g[l], ln1_b[l], alpha)
        x = _ffn_call(x, mod[3], mod[4], mod[5], *ffn_w, ln2_g[l], ln2_b[l], alpha)
        if not last:
            kre_c, kim_c, knyq_c = _hyena_filters(lc, *hy, dft_c)
            yb_c = _hyena_call(ph_c, dft_c[0], dft_c[2], kre_c, kim_c, knyq_c, hyena_conv_w[l], hyena_conv_b[l],
                               hyena_d[l], gs(1, ng), gs(1, nb))
            xc = _outproj_call((ya_c, yb_c, yr_c, yw_c), xc, modc[2], w_out_bf, ln1_g[l], ln1_b[l], alpha)
            xc = _ffn_call(xc, modc[3], modc[4], modc[5], *ffn_w, ln2_g[l], ln2_b[l], alpha)
    return x
```

```python
import functools
import math

import numpy as np
import jax
import jax.numpy as jnp
from jax import lax
from jax.experimental import pallas as pl
from jax.experimental.pallas import tpu as pltpu

F32 = jnp.float32
BF16 = jnp.bfloat16

HEAD_DIM = 64
GROUP_W = 256
N_HEADS_G = 4
CHUNK = 128
RWKV_CHUNK = 64
GRID_W = 64
ROPE_PAIRS = 16
ROPE_BASE = 10000.0
HYENA_EMB = 33
HYENA_BANDS = 16
HYENA_FAST_DECAY = 0.3
HYENA_SLOW_DECAY = 1.5
HYENA_TARGET = 1e-2
LN_EPS = 1e-5
GN_EPS = 1e-5
LANES = 128
SUBLANES = 8
PACKED_ROWS = 16
ACT = jnp.bfloat16
MLSTM_NB = 2
RWKV_NB = 4
RET_NB = 2
FFN_ROWS = 1024
PROJ_ROWS = 512
VMEM_LIMIT = 56 * 1024 * 1024

NN = (((1,), (0,)), ((), ()))
NT = (((1,), (1,)), ((), ()))
TN = (((0,), (0,)), ((), ()))


def _dg(a, b, dims=NN):
    return lax.dot_general(a, b, dims, preferred_element_type=F32)


def _bf(x):
    return x.astype(BF16)


def _split2(x):
    hi = _bf(x)
    return hi, _bf(x - hi.astype(F32))


def _split3(x):
    hi = _bf(x)
    r = x - hi.astype(F32)
    mid = _bf(r)
    return hi, mid, _bf(r - mid.astype(F32))


def _mm(a, b, dims=NN, passes=1):
    if passes == 1:
        return _dg(_bf(a), _bf(b), dims)
    ah, al = _split2(a)
    bh, bl = _split2(b)
    return _dg(ah, bh, dims) + (_dg(al, bh, dims) + _dg(ah, bl, dims))


def _mm_xl(a_exact, b, dims=NN):
    b1, b2, b3 = _split3(b)
    return _dg(a_exact, b1, dims) + (_dg(a_exact, b2, dims) + _dg(a_exact, b3, dims))


def _mm_xr(a, b_exact, dims=NN):
    a1, a2 = _split2(a)
    return _dg(a1, b_exact, dims) + _dg(a2, b_exact, dims)


def _iota(shape, dim):
    return lax.broadcasted_iota(jnp.int32, shape, dim)


def _tri_mask(t, rev, strict=False):
    r = _iota((t, t), 0)
    c = _iota((t, t), 1)
    if strict:
        return (c > r) if rev else (c < r)
    return (c >= r) if rev else (c <= r)


def _head_ones():
    r = _iota((GROUP_W, GROUP_W), 0)
    c = _iota((GROUP_W, GROUP_W), 1)
    return jnp.where((r >> 6) == (c >> 6), 1.0, 0.0).astype(BF16)


def _head_sum(x, ones_bd):
    return _dg(_bf(x), ones_bd)


def _cumsum_rows(x, rev):
    t = x.shape[0]
    rowi = _iota(x.shape, 0)
    sft = 1
    while sft < t:
        if rev:
            prev = jnp.where(rowi < t - sft, pltpu.roll(x, t - sft, 0), 0.0)
        else:
            prev = jnp.where(rowi >= sft, pltpu.roll(x, sft, 0), 0.0)
        x = x + prev
        sft *= 2
    return x


def _head_norm(y, ones_bd, g, b):
    mu = _head_sum(y, ones_bd) * (1.0 / HEAD_DIM)
    yc = y - mu
    var = _head_sum(yc * yc, ones_bd) * (1.0 / HEAD_DIM)
    return yc * lax.rsqrt(var + GN_EPS) * g + b


def _ln_rows(x):
    mu = jnp.mean(x, -1, keepdims=True)
    xc = x - mu
    var = jnp.mean(xc * xc, -1, keepdims=True)
    return xc * lax.rsqrt(var + LN_EPS)


def _sigmoid(x):
    return 1.0 / (1.0 + jnp.exp(-x))


def _log_sigmoid(x):
    return jnp.minimum(x, 0.0) - jnp.log(1.0 + jnp.exp(-jnp.abs(x)))


def _with_neighbors(ref, s, t, ls, c0, c1, bi=0):
    grp = PACKED_ROWS
    cur = ref[bi, pl.ds(s, t), c0:c1].astype(F32)
    p0 = pl.multiple_of(jnp.maximum(s - grp, 0), grp)
    n0 = pl.multiple_of(jnp.minimum(s + t, ls - grp), grp)
    prow = ref[bi, pl.ds(p0, grp), c0:c1].astype(F32)[grp - 1:grp]
    nrow = ref[bi, pl.ds(n0, grp), c0:c1].astype(F32)[0:1]
    prow = jnp.where(s > 0, prow, 0.0)
    nrow = jnp.where(s + t < ls, nrow, 0.0)
    rows = _iota((t, 1), 0)
    prev = jnp.where(rows == 0, prow, pltpu.roll(cur, 1, 0))
    nxt = jnp.where(rows == t - 1, nrow, pltpu.roll(cur, t - 1, 0))
    return cur, prev, nxt


def _dwconv_rows(ref, s, t, ls, c0, c1, w_ref, b_ref, bi=0):
    cur, prev, nxt = _with_neighbors(ref, s, t, ls, c0, c1, bi)
    return prev * w_ref[0:1, c0:c1] + cur * w_ref[1:2, c0:c1] + nxt * w_ref[2:3, c0:c1] + b_ref[0:1, c0:c1]


def _cparams(sem):
    return pltpu.CompilerParams(dimension_semantics=sem, vmem_limit_bytes=VMEM_LIMIT)


def _const_spec(shape):
    nd = len(shape)
    return pl.BlockSpec(shape, lambda *_: (0,) * nd, pipeline_mode=pl.Buffered(1))


def _seq_spec(ls, w, nbat=1, single=False):
    if single:
        return pl.BlockSpec((nbat, ls, w), lambda b: (b, 0, 0), pipeline_mode=pl.Buffered(1))
    return pl.BlockSpec((nbat, ls, w), lambda b: (b, 0, 0))


def _mix_nb(bsz, want):
    return want if bsz % want == 0 else 1


def _ada_kernel(c_ref, w_ref, b_ref, o_ref):
    c = c_ref[...]
    o_ref[...] = _mm(c * _sigmoid(c), w_ref[...], passes=3) + b_ref[...]


def _ada_call(cc, w, b):
    m, d = cc.shape
    n = w.shape[1]
    tn = 512
    return pl.pallas_call(
        _ada_kernel,
        grid=(n // tn,),
        in_specs=[pl.BlockSpec((m, d), lambda j: (0, 0)),
                  pl.BlockSpec((d, tn), lambda j: (0, j)),
                  pl.BlockSpec((1, tn), lambda j: (0, j))],
        out_specs=pl.BlockSpec((m, tn), lambda j: (0, j)),
        out_shape=jax.ShapeDtypeStruct((m, n), F32),
        compiler_params=_cparams(("arbitrary",)),
        name="ada_mod",
    )(cc, w, b.reshape(1, n))


SEG_W = (1152, 768, 1024, 896)


def _inproj_kernel(x_ref, sh_ref, sc_ref, w_ref, o1, o2, o3, o4):
    u = _ln_rows(x_ref[0]) * (1.0 + sc_ref[0]) + sh_ref[0]
    ub = _bf(u)
    off = 0
    for o in (o1, o2, o3, o4):
        n = o.shape[-1]
        o[0] = _dg(ub, w_ref[:, off:off + n]).astype(o.dtype)
        off += n


def _mod_spec(arr):
    d = arr.shape[-1]
    if arr.shape[0] == 1:
        return pl.BlockSpec((1, 1, d), lambda b, i: (0, 0, 0))
    return pl.BlockSpec((1, 1, d), lambda b, i: (b, 0, 0))


def _inproj_call(x, shift, scale, w_bf):
    bsz, ls, d = x.shape
    tm = min(PROJ_ROWS, ls)
    return pl.pallas_call(
        _inproj_kernel,
        grid=(bsz, ls // tm),
        in_specs=[pl.BlockSpec((1, tm, d), lambda b, i: (b, i, 0)),
                  _mod_spec(shift), _mod_spec(scale),
                  _const_spec(w_bf.shape)],
        out_specs=[pl.BlockSpec((1, tm, n), lambda b, i: (b, i, 0)) for n in SEG_W],
        out_shape=[jax.ShapeDtypeStruct((bsz, ls, n), ACT) for n in SEG_W],
        compiler_params=_cparams(("parallel", "parallel")),
        name="in_proj",
    )(x, shift, scale, w_bf)


def _mlstm_solve(chunks, states):
    t = chunks[0][0].shape[0]
    hd = HEAD_DIM
    assert t == LANES
    lane = _iota((t, LANES), 1)
    rowi = _iota((t, LANES), 0)
    lo = lane < hd
    chains = []
    for di, (q, k, v, gates, rev) in enumerate(chunks):
        d = 1 if rev else 0
        mask = _tri_mask(t, rev)
        tri = jnp.where(mask, 1.0, 0.0).astype(BF16)
        bc = _mm_xl(tri, _log_sigmoid(gates))
        gates_t = gates.T
        bc_t = bc.T
        for h in range(N_HEADS_G):
            p, e = divmod(h, 2)
            ci, cf = 8 * d + h, 8 * d + 4 + h
            cols = slice(LANES * p, LANES * (p + 1))
            half = lo if e == 0 else jnp.logical_not(lo)
            b_rep = jnp.broadcast_to(bc[:, cf:cf + 1], (t, LANES))
            c_rep = jnp.broadcast_to(gates[:, ci:ci + 1], (t, LANES)) - b_rep
            chains.append(dict(
                di=di, h=h, p=p, e=e, half=half, mask=mask, rev=rev, last=0 if rev else t - 1,
                q=q[:, cols], k=k[:, cols], v=v[:, cols], b_rep=b_rep, c_rep=c_rep,
                c_row=gates_t[ci:ci + 1, :] - bc_t[cf:cf + 1, :],
                cn=states[di][0][p], m=states[di][1][h]))
    for c in chains:
        c["qm"] = jnp.where(c["half"], c["q"], 0.0)
        c["s_raw"] = _dg(_bf(c["qm"]), _bf(c["k"]), NT)
    for c in chains:
        cm = c["c_rep"]
        sft = 1
        while sft < t:
            if c["rev"]:
                prev = jnp.where(rowi < t - sft, pltpu.roll(cm, t - sft, 0), -jnp.inf)
            else:
                prev = jnp.where(rowi >= sft, pltpu.roll(cm, sft, 0), -jnp.inf)
            cm = jnp.maximum(cm, prev)
            sft *= 2
        c["big_m"] = jnp.maximum(cm, c["m"])
        c["s"] = c["s_raw"] * jnp.exp(jnp.where(c["mask"], c["c_row"] - c["big_m"], -jnp.inf))
        c["w_inter"] = jnp.exp(c["m"] - c["big_m"])
    for c in chains:
        c["v_aug"] = _bf(jnp.where(c["half"], c["v"], 1.0))
        lhs = jnp.concatenate([_bf(c["s"]), _bf(c["qm"] * c["w_inter"])], axis=1)
        rhs = jnp.concatenate([c["v_aug"], _bf(c["cn"])], axis=0)
        c["res"] = _dg(lhs, rhs)
    for c in chains:
        den = pltpu.roll(c["res"], hd, 1)
        c["hh"] = c["res"] / jnp.maximum(jnp.abs(den), jnp.exp(-(c["b_rep"] + c["big_m"])))
    for c in chains:
        last = c["last"]
        m_last = c["big_m"][last:last + 1, :]
        c["m_new"] = c["b_rep"][last:last + 1, :] + m_last
        c["kw"] = _bf(jnp.where(c["half"], c["k"], 0.0) * jnp.exp(c["c_rep"] - m_last))
        c["decay"] = jnp.exp(c["m"] - m_last)
    out = []
    row = _iota((LANES, LANES), 0)
    for di in range(len(chunks)):
        hs, cns = [], []
        for p in range(N_HEADS_G // 2):
            ce, co = [c for c in chains if c["di"] == di and c["p"] == p]
            hs.append(jnp.where(lo, ce["hh"], co["hh"]))
            upd = _dg(jnp.concatenate([ce["kw"], co["kw"]], axis=0),
                      jnp.concatenate([ce["v_aug"], co["v_aug"]], axis=0), TN)
            cns.append(jnp.where(row < hd, ce["decay"], co["decay"]) * ce["cn"] + upd)
        ms = tuple(c["m_new"] for c in chains if c["di"] == di)
        out.append((jnp.concatenate(hs, axis=-1), (tuple(cns), ms)))
    return out


def _mlstm_kernel(p_ref, pc_ref, cw_ref, cb_ref, gb_ref, ng_ref, nb_ref, o_ref, oc_ref,
                  qk_ref, qkc_ref, hf_ref, hb_ref, hfc_ref, hbc_ref):
    t = CHUNK
    nbat = p_ref.shape[0]
    ones_bd = _head_ones()

    def conv_pass(src, dst):
        ls = src.shape[1]

        def body(c, carry):
            s = pl.multiple_of(c * t, t)
            for bi in range(nbat):
                y = _dwconv_rows(src, s, t, ls, 0, 2 * GROUP_W, cw_ref, cb_ref, bi)
                dst[bi, pl.ds(s, t), :] = (y * _sigmoid(y)).astype(dst.dtype)
            return carry

        lax.fori_loop(0, ls // t, body, 0)

    conv_pass(p_ref, qk_ref)
    conv_pass(pc_ref, qkc_ref)

    def scan(src, qk, hf, hb, states):
        ls = src.shape[1]
        n = ls // t

        def body(i, carry):
            starts = (pl.multiple_of(i * t, t), pl.multiple_of((n - 1 - i) * t, t))
            chunks = []
            for bi in range(nbat):
                for s, rev in zip(starts, (False, True)):
                    q = qk[bi, pl.ds(s, t), 0:GROUP_W].astype(F32) * (HEAD_DIM ** -0.5)
                    k = qk[bi, pl.ds(s, t), GROUP_W:2 * GROUP_W].astype(F32)
                    v = src[bi, pl.ds(s, t), 2 * GROUP_W:3 * GROUP_W].astype(F32)
                    gates = src[bi, pl.ds(s, t), 4 * GROUP_W:4 * GROUP_W + LANES].astype(F32) + gb_ref[...]
                    chunks.append((q, k, v, gates, rev))
            res = _mlstm_solve(chunks, carry)
            for bi in range(nbat):
                hf[bi, pl.ds(starts[0], t), :] = res[2 * bi][0]
                hb[bi, pl.ds(starts[1], t), :] = res[2 * bi + 1][0]
            return tuple(r[1] for r in res)

        return lax.fori_loop(0, n, body, states)

    zero = (tuple(jnp.zeros((LANES, LANES), F32) for _ in range(N_HEADS_G // 2)),
            tuple(jnp.zeros((1, LANES), F32) for _ in range(N_HEADS_G)))
    states = scan(pc_ref, qkc_ref, hfc_ref, hbc_ref, (zero,) * (2 * nbat))
    scan(p_ref, qk_ref, hf_ref, hb_ref, states)

    def finish(src, hf, hb, out):
        ls = src.shape[1]

        def body(c, carry):
            s = pl.multiple_of(c * t, t)
            for bi in range(nbat):
                h = hf[bi, pl.ds(s, t), :] + hb[bi, pl.ds(s, t), :]
                o = src[bi, pl.ds(s, t), 3 * GROUP_W:4 * GROUP_W].astype(F32)
                y = _sigmoid(o) * _head_norm(h, ones_bd, ng_ref[...], nb_ref[...])
                out[bi, pl.ds(s, t), :] = y.astype(out.dtype)
            return carry

        lax.fori_loop(0, ls // t, body, 0)

    finish(p_ref, hf_ref, hb_ref, o_ref)
    finish(pc_ref, hfc_ref, hbc_ref, oc_ref)


def _mixer_out(bsz, ls, lc, nbat=1):
    return ([_seq_spec(ls, GROUP_W, nbat), _seq_spec(lc, GROUP_W, nbat)],
            [jax.ShapeDtypeStruct((bsz, ls, GROUP_W), ACT), jax.ShapeDtypeStruct((bsz, lc, GROUP_W), ACT)])


def _mlstm_call(p, pc, conv_w, conv_b, gate_b, ng, nb):
    bsz, ls, w = p.shape
    lc = pc.shape[1]
    nbat = _mix_nb(bsz, MLSTM_NB)
    gb = jnp.zeros((1, LANES), F32).at[0, :4 * N_HEADS_G].set(gate_b.reshape(-1))
    consts = [conv_w, conv_b.reshape(1, -1), gb, ng.reshape(1, -1), nb.reshape(1, -1)]
    out_specs, out_shape = _mixer_out(bsz, ls, lc, nbat)
    return pl.pallas_call(
        _mlstm_kernel,
        grid=(bsz // nbat,),
        in_specs=[_seq_spec(ls, w, nbat), _seq_spec(lc, w, nbat)] + [_const_spec(a.shape) for a in consts],
        out_specs=out_specs,
        out_shape=out_shape,
        scratch_shapes=[pltpu.VMEM((nbat, ls, 2 * GROUP_W), BF16), pltpu.VMEM((nbat, lc, 2 * GROUP_W), BF16),
                        pltpu.VMEM((nbat, ls, GROUP_W), F32), pltpu.VMEM((nbat, ls, GROUP_W), F32),
                        pltpu.VMEM((nbat, lc, GROUP_W), F32), pltpu.VMEM((nbat, lc, GROUP_W), F32)],
        compiler_params=_cparams(("parallel",)),
        name="mlstm_mixer",
    )(p, pc, *consts)


def _ret_log_gamma(h, rev):
    hh = (N_HEADS_G - 1 - h) if rev else h
    return math.log(1.0 - 2.0 ** (-5.0 - hh))


def _ret_solve(chunks, states):
    t = chunks[0][0].shape[0]
    hd = HEAD_DIM
    lo = _iota((t, LANES), 1) < hd
    r = _iota((t, t), 0)
    cc = _iota((t, t), 1)
    work = []
    for di, (q, k, v, rev) in enumerate(chunks):
        diff = ((cc - r) if rev else (r - cc)).astype(F32)
        valid = diff >= 0.0
        ordinal = _iota((t, 1), 0).astype(F32)
        if rev:
            ordinal = (t - 1.0) - ordinal
        for p in range(N_HEADS_G // 2):
            cols = slice(LANES * p, LANES * (p + 1))
            lg = [_ret_log_gamma(2 * p + e, rev) for e in range(2)]
            work.append(dict(di=di, p=p, q=q[:, cols], k=k[:, cols], v=v[:, cols], lg=lg, diff=diff, valid=valid,
                             ordinal=ordinal, s0=states[di][p]))
    for w in work:
        kb = _bf(w["k"])
        w["qm"] = [jnp.where(lo if e == 0 else jnp.logical_not(lo), w["q"], 0.0) for e in range(2)]
        w["s_raw"] = [_dg(_bf(w["qm"][e]), kb, NT) for e in range(2)]
    for w in work:
        dm = [jnp.where(w["valid"], jnp.exp(jnp.where(w["valid"], w["diff"], 0.0) * w["lg"][e]), 0.0) for e in range(2)]
        q_dec = jnp.where(lo, jnp.exp((w["ordinal"] + 1.0) * w["lg"][0]), jnp.exp((w["ordinal"] + 1.0) * w["lg"][1]))
        k_dec = jnp.where(lo, jnp.exp((t - 1.0 - w["ordinal"]) * w["lg"][0]),
                          jnp.exp((t - 1.0 - w["ordinal"]) * w["lg"][1]))
        vb = _bf(w["v"])
        zero = jnp.zeros_like(vb)
        vm = [jnp.where(lo, vb, zero), jnp.where(lo, zero, vb)]
        lhs = jnp.concatenate([_bf(w["s_raw"][0] * dm[0]), _bf(w["s_raw"][1] * dm[1]), _bf(w["q"] * q_dec)], axis=1)
        rhs = jnp.concatenate([vm[0], vm[1], _bf(w["s0"])], axis=0)
        w["o"] = _dg(lhs, rhs)
        kd = _bf(w["k"] * k_dec)
        kz = jnp.zeros_like(kd)
        upd = _dg(jnp.concatenate([jnp.where(lo, kd, kz), jnp.where(lo, kz, kd)], axis=0),
                  jnp.concatenate(vm, axis=0), TN)
        rows = _iota((LANES, 1), 0)
        cd = jnp.where(rows < hd, math.exp(t * w["lg"][0]), math.exp(t * w["lg"][1]))
        w["s_new"] = w["s0"] * cd + upd
    out = []
    for di in range(len(chunks)):
        mine = [w for w in work if w["di"] == di]
        out.append((jnp.concatenate([w["o"] for w in mine], axis=-1), tuple(w["s_new"] for w in mine)))
    return out


def _rope(x, cos_t, sin_t):
    w = x.shape[-1]
    lane = _iota(x.shape, 1)
    partner = jnp.where((lane & 31) < ROPE_PAIRS, pltpu.roll(x, w - ROPE_PAIRS, 1), pltpu.roll(x, ROPE_PAIRS, 1))
    return x * cos_t + partner * sin_t


def _ret_kernel(p_ref, pc_ref, cos_ref, sin_ref, ng_ref, nb_ref, o_ref, oc_ref,
                hf_ref, hb_ref, hfc_ref, hbc_ref):
    t = CHUNK
    nbat = p_ref.shape[0]
    ones_bd = _head_ones()

    def scan(src, hf, hb, states, rotate):
        ls = src.shape[1]
        n = ls // t

        def body(i, carry):
            starts = (pl.multiple_of(i * t, t), pl.multiple_of((n - 1 - i) * t, t))
            chunks = []
            for bi in range(nbat):
                for s, rev in zip(starts, (False, True)):
                    q = src[bi, pl.ds(s, t), 0:GROUP_W].astype(F32) * (HEAD_DIM ** -0.5)
                    k = src[bi, pl.ds(s, t), GROUP_W:2 * GROUP_W].astype(F32)
                    v = src[bi, pl.ds(s, t), 2 * GROUP_W:3 * GROUP_W].astype(F32)
                    if rotate:
                        cos_t = cos_ref[pl.ds(s, t), :]
                        sin_t = sin_ref[pl.ds(s, t), :]
                        q = _rope(q, cos_t, sin_t)
                        k = _rope(k, cos_t, sin_t)
                    chunks.append((q, k, v, rev))
            res = _ret_solve(chunks, carry)
            for bi in range(nbat):
                hf[bi, pl.ds(starts[0], t), :] = res[2 * bi][0]
                hb[bi, pl.ds(starts[1], t), :] = res[2 * bi + 1][0]
            return tuple(r[1] for r in res)

        return lax.fori_loop(0, n, body, states)

    zero = tuple(jnp.zeros((LANES, LANES), F32) for _ in range(N_HEADS_G // 2))
    states = scan(pc_ref, hfc_ref, hbc_ref, (zero,) * (2 * nbat), False)
    scan(p_ref, hf_ref, hb_ref, states, True)

    def finish(src, hf, hb, out):
        ls = src.shape[1]

        def body(c, carry):
            s = pl.multiple_of(c * t, t)
            for bi in range(nbat):
                h = hf[bi, pl.ds(s, t), :] + hb[bi, pl.ds(s, t), :]
                g = src[bi, pl.ds(s, t), 3 * GROUP_W:4 * GROUP_W].astype(F32)
                y = g * _sigmoid(g) * _head_norm(h, ones_bd, ng_ref[...], nb_ref[...])
                out[bi, pl.ds(s, t), :] = y.astype(out.dtype)
            return carry

        lax.fori_loop(0, ls // t, body, 0)

    finish(p_ref, hf_ref, hb_ref, o_ref)
    finish(pc_ref, hfc_ref, hbc_ref, oc_ref)


def _rope_tables(ls):
    pos = np.arange(ls)
    freqs = ROPE_BASE ** (-np.arange(ROPE_PAIRS, dtype=np.float64) / ROPE_PAIRS)
    row = (pos // GRID_W).astype(np.float64)[:, None] * freqs
    col = (pos % GRID_W).astype(np.float64)[:, None] * freqs
    ang = np.concatenate([row, row, col, col], -1)
    sign = np.tile(np.concatenate([-np.ones(ROPE_PAIRS), np.ones(ROPE_PAIRS)]), 2)
    cos_t = np.tile(np.cos(ang), (1, N_HEADS_G)).astype(np.float32)
    sin_t = np.tile(np.sin(ang) * sign, (1, N_HEADS_G)).astype(np.float32)
    return cos_t, sin_t


def _ret_call(p, pc, ng, nb):
    bsz, ls, w = p.shape
    lc = pc.shape[1]
    nbat = _mix_nb(bsz, RET_NB)
    cos_t, sin_t = _rope_tables(ls)
    consts = [cos_t, sin_t, ng.reshape(1, -1), nb.reshape(1, -1)]
    out_specs, out_shape = _mixer_out(bsz, ls, lc, nbat)
    return pl.pallas_call(
        _ret_kernel,
        grid=(bsz // nbat,),
        in_specs=[_seq_spec(ls, w, nbat), _seq_spec(lc, w, nbat)] + [_const_spec(a.shape) for a in consts],
        out_specs=out_specs,
        out_shape=out_shape,
        scratch_shapes=[pltpu.VMEM((nbat, ls, GROUP_W), F32), pltpu.VMEM((nbat, ls, GROUP_W), F32),
                        pltpu.VMEM((nbat, lc, GROUP_W), F32), pltpu.VMEM((nbat, lc, GROUP_W), F32)],
        compiler_params=_cparams(("parallel",)),
        name="retention_mixer",
    )(p, pc, *consts)


RW_INV_PASSES = 1


def _rwkv_shifted(src, s, t, mu_ref, bi=0):
    ls = src.shape[1]
    w = src.shape[2]
    cur, prev, nxt = _with_neighbors(src, s, t, ls, 0, w, bi)
    return cur + mu_ref[0:1, :] * (prev - cur) + mu_ref[1:2, :] * (nxt - cur)


def _rwkv_prep(z, prm, rev, ones_bd):
    w0_ref, w2_ref, a0_ref, a2_ref, kks_ref, ka_ref = prm
    t = z.shape[0]
    d = 1 if rev else 0
    g = GROUP_W
    r, k, v = z[:, 0:g], z[:, g:2 * g], z[:, 2 * g:3 * g]
    wlo = z[:, 3 * g:3 * g + 32]
    alo = z[:, 3 * g + 32:3 * g + 64]
    kk = k * kks_ref[...]
    kk = kk * lax.rsqrt(_head_sum(kk * kk, ones_bd) + 1e-12)
    w_pre = w0_ref[d:d + 1, :] + _mm(jnp.tanh(wlo), w2_ref[d])
    sp = jnp.maximum(-w_pre, 0.0) + jnp.log(1.0 + jnp.exp(-jnp.abs(w_pre)))
    logw = -jnp.exp(-sp - 0.5)
    ag = _sigmoid(a0_ref[d:d + 1, :] + _mm(alo, a2_ref[d]))
    kd = k * (1.0 + (ag - 1.0) * ka_ref[...])
    a = -kk
    b = kk * ag
    cum = _cumsum_rows(logw, rev)
    e_pos = jnp.exp(cum)
    e_neg = jnp.exp(-cum)
    at = a * jnp.exp(cum - logw)
    rt = r * e_pos
    bt = b * e_neg
    kt = kd * e_neg
    last = 0 if rev else t - 1
    return dict(at=_bf(at), rt=_bf(rt), bt=_bf(bt), kt=_bf(kt), v=_bf(v), w_last=e_pos[last:last + 1, :], rev=rev)


def _rwkv_solve(preps, states):
    t = preps[0]["v"].shape[0]
    hd = HEAD_DIM
    t2 = 2 * t
    n_lvl = int(math.log2(t))
    lo = _iota((t, LANES), 1) < hd
    rowi = _iota((t2, 2 * t2), 0) & (t - 1)
    colj = _iota((t2, 2 * t2), 1) & (t - 1)

    def expand(x):
        zero = jnp.zeros_like(x)
        return jnp.concatenate([jnp.where(lo, x, zero), jnp.where(lo, zero, x)], axis=0)

    work = []
    for di, pr in enumerate(preps):
        strict2 = (colj > rowi) if pr["rev"] else (colj < rowi)
        incl2 = (colj >= rowi) if pr["rev"] else (colj <= rowi)
        for p in range(N_HEADS_G // 2):
            cols = slice(LANES * p, LANES * (p + 1))
            ex = {name: expand(pr[name][:, cols]) for name in ("at", "rt", "bt", "kt", "v")}
            work.append(dict(di=di, p=p, ar=jnp.concatenate([ex["at"], ex["rt"]], axis=0),
                             bk=jnp.concatenate([ex["bt"], ex["kt"]], axis=0), v=ex["v"],
                             s0=states[di][p], w_last=pr["w_last"][:, cols], strict2=strict2, incl2=incl2))
    for w in work:
        rhs = jnp.concatenate([w["bk"], _bf(w["s0"])], axis=0)
        w["gh"] = _dg(w["ar"], rhs, NT)
    for w in work:
        gh = w["gh"]
        w["lo"] = jnp.where(w["strict2"], gh[:t2, :2 * t2], 0.0)
        w["m"] = _bf(jnp.where(w["incl2"], gh[t2:, :2 * t2], 0.0))
    for w in work:
        w["u0"] = w["gh"][:t2, 2 * t2:] + _dg(_bf(w["lo"][:, t2:]), w["v"])
    diff = _iota((t2, t2), 0) ^ _iota((t2, t2), 1)
    eye = jnp.where(diff == 0, 1.0, 0.0)
    for w in work:
        w["l"] = w["lo"][:, :t2]
        w["d"] = eye + jnp.where(diff == 1, w["l"], 0.0)
    for k in range(1, n_lvl):
        join = (diff >> k) == 1
        for w in work:
            w["e"] = _mm(jnp.where(join, w["l"], 0.0), w["d"], passes=RW_INV_PASSES)
        for w in work:
            w["d"] = w["d"] + _mm(w["d"], w["e"], passes=RW_INV_PASSES)
    for w in work:
        w["uv"] = jnp.concatenate([_bf(_mm(w["d"], w["u0"], passes=RW_INV_PASSES)), w["v"]], axis=0)
    for w in work:
        y_bd = w["gh"][t2:, 2 * t2:] + _dg(w["m"], w["uv"])
        w["y"] = y_bd[:t] + y_bd[t:]
    for w in work:
        w["s_new"] = (w["s0"] + _dg(w["uv"], w["bk"], TN)) * w["w_last"]
    out = []
    for di in range(len(preps)):
        mine = [w for w in work if w["di"] == di]
        out.append((jnp.concatenate([w["y"] for w in mine], axis=-1), tuple(w["s_new"] for w in mine)))
    return out


def _rwkv_kernel(p_ref, pc_ref, mu_ref, w0_ref, w2_ref, a0_ref, a2_ref, g2_ref, kks_ref, ka_ref, rk_ref,
                 ng_ref, nb_ref, o_ref, oc_ref, hf_ref, hb_ref, hfc_ref, hbc_ref):
    t = RWKV_CHUNK
    nbat = p_ref.shape[0]
    ones_bd = _head_ones()
    prm = (w0_ref, w2_ref, a0_ref, a2_ref, kks_ref, ka_ref)

    def scan(src, hf, hb, states):
        ls = src.shape[1]
        n = ls // t

        def body(i, carry):
            starts = (pl.multiple_of(i * t, t), pl.multiple_of((n - 1 - i) * t, t))
            preps = [_rwkv_prep(_rwkv_shifted(src, s, t, mu_ref, bi), prm, rev, ones_bd)
                     for bi in range(nbat) for s, rev in zip(starts, (False, True))]
            res = _rwkv_solve(preps, carry)
            for bi in range(nbat):
                hf[bi, pl.ds(starts[0], t), :] = res[2 * bi][0].astype(hf.dtype)
                hb[bi, pl.ds(starts[1], t), :] = res[2 * bi + 1][0].astype(hb.dtype)
            return tuple(r[1] for r in res)

        return lax.fori_loop(0, n, body, states)

    zero = tuple(jnp.zeros((LANES, LANES), F32) for _ in range(N_HEADS_G // 2))
    states = scan(pc_ref, hfc_ref, hbc_ref, (zero,) * (2 * nbat))
    scan(p_ref, hf_ref, hb_ref, states)

    def finish(src, hf, hb, out):
        ls = src.shape[1]
        g = GROUP_W

        def body(c, carry):
            s = pl.multiple_of(c * t, t)
            for bi in range(nbat):
                z = _rwkv_shifted(src, s, t, mu_ref, bi)
                r, k, v = z[:, 0:g], z[:, g:2 * g], z[:, 2 * g:3 * g]
                glo = z[:, 3 * g + 64:3 * g + 128]
                gate = _mm(_sigmoid(glo), g2_ref[...])
                bonus = _head_sum(r * k * rk_ref[...], ones_bd) * v
                h = hf[bi, pl.ds(s, t), :].astype(F32) + hb[bi, pl.ds(s, t), :].astype(F32)
                y = gate *(_head_norm(h, ones_bd, ng_ref[...], nb_ref[...]) + bonus)
                out[bi, pl.ds(s, t), :] = y.astype(out.dtype)
            return carry

        lax.fori_loop(0, ls // t, body, 0)

    finish(p_ref, hf_ref, hb_ref, o_ref)
    finish(pc_ref, hfc_ref, hbc_ref, oc_ref)


def _rwkv_call(p, pc, mu, w0, w2, a0, a2, g2, kk, ka, rk, ng, nb):
    bsz, ls, w = p.shape
    lc = pc.shape[1]
    nbat = _mix_nb(bsz, RWKV_NB)
    consts = [mu, w0, w2, a0, a2, g2, kk.reshape(1, -1), ka.reshape(1, -1), rk.reshape(1, -1),
              ng.reshape(1, -1), nb.reshape(1, -1)]
    out_specs, out_shape = _mixer_out(bsz, ls, lc, nbat)
    return pl.pallas_call(
        _rwkv_kernel,
        grid=(bsz // nbat,),
        in_specs=[_seq_spec(ls, w, nbat, single=True), _seq_spec(lc, w, nbat)] + [_const_spec(a.shape) for a in consts],
        out_specs=out_specs,
        out_shape=out_shape,
        scratch_shapes=[pltpu.VMEM((nbat, ls, GROUP_W), ACT), pltpu.VMEM((nbat, ls, GROUP_W), ACT),
                        pltpu.VMEM((nbat, lc, GROUP_W), ACT), pltpu.VMEM((nbat, lc, GROUP_W), ACT)],
        compiler_params=_cparams(("parallel",)),
        name="rwkv7_mixer",
    )(p, pc, *consts)


def _dft_mats(ls):
    n = 2 * ls
    idx = np.arange(ls, dtype=np.int64)
    prod = (idx[:, None] * idx[None, :]) % n
    ang = 2.0 * np.pi * prod.astype(np.float64) / n
    return np.cos(ang), -np.sin(ang)


def _hyena_feats(ls):
    pos = np.arange(ls, dtype=np.float64)
    tt = np.linspace(0.0, 1.0, ls)[:, None]
    ang = 2.0 * math.pi * pos[:, None] / ls
    bands = np.linspace(1e-4, HYENA_BANDS - 1, HYENA_BANDS)[None, :]
    feats = np.concatenate([tt, np.cos(bands * ang), -np.sin(bands * ang)], -1)
    feats = np.pad(feats, ((0, 0), (0, LANES - HYENA_EMB))).astype(np.float32)
    max_decay = math.log(HYENA_TARGET) / HYENA_FAST_DECAY
    min_decay = math.log(HYENA_TARGET) / HYENA_SLOW_DECAY
    deltas = np.abs(np.linspace(min_decay, max_decay, GROUP_W))
    window = np.exp(-tt * deltas).astype(np.float32)
    return feats, window


def _hyena_taps_kernel(f_ref, win_ref, w1_ref, b1_ref, w2_ref, b2_ref, w3_ref, fr_ref, sum_ref, dif_ref, nyq_ref):
    fr = fr_ref[...]
    h = jnp.sin(fr * (_mm(f_ref[...], w1_ref[...], passes=3) + b1_ref[...]))
    h = jnp.sin(fr * (_mm(h, w2_ref[...], passes=3) + b2_ref[...]))
    h = _mm(h, w3_ref[...], passes=3)
    win = win_ref[...]
    win2 = jnp.concatenate([win, win], axis=-1)
    hf = h[:, 0:2 * GROUP_W] * win2
    hb = h[:, 2 * GROUP_W:4 * GROUP_W] * win2
    sum_ref[...] = hf + hb
    dif_ref[...] = hf - hb
    ls = h.shape[0]
    alt = 1.0 - 2.0 * (_iota((ls, 1), 0) & 1).astype(F32)
    nyq_ref[...] = jnp.sum((hf + hb) * alt, axis=0, keepdims=True) * (1.0 / (2.0 * ls))


def _hyena_spec_kernel(fch_ref, fcl_ref, fsh_ref, fsl_ref, sum_ref, dif_ref, kre_ref, kim_ref):
    j = pl.program_id(0)
    tf = fch_ref.shape[0]
    ls = fch_ref.shape[1]
    hs_h, hs_l = _split2(sum_ref[...])
    hd_h, hd_l = _split2(dif_ref[...])
    kre = _dg(fch_ref[...], hs_h) + (_dg(fcl_ref[...], hs_h) + _dg(fch_ref[...], hs_l))
    kim = _dg(fsh_ref[...], hd_h) + (_dg(fsl_ref[...], hd_h) + _dg(fsh_ref[...], hd_l))
    f_idx = _iota((tf, 1), 0) + j * tf
    scale = jnp.where(f_idx == 0, 1.0, 2.0) * (1.0 / (2.0 * ls))
    kre_ref[...] = kre * scale
    kim_ref[...] = kim * scale


def _hyena_filters(ls, w1, b1, w2, b2, w3, freq, fmats):
    feats, window = _hyena_feats(ls)
    w1p = jnp.pad(w1, ((0, LANES - HYENA_EMB), (0, 0)))
    args = [jnp.asarray(feats), jnp.asarray(window), w1p, b1.reshape(1, -1), w2, b2.reshape(1, -1), w3,
            freq.reshape(1, -1)]
    hsum, hdif, knyq = pl.pallas_call(
        _hyena_taps_kernel,
        out_shape=[jax.ShapeDtypeStruct((ls, 2 * GROUP_W), F32)] * 2 + [jax.ShapeDtypeStruct((1, 2 * GROUP_W), F32)],
        compiler_params=pltpu.CompilerParams(vmem_limit_bytes=VMEM_LIMIT),
        name="hyena_taps",
    )(*args)
    fch, fcl, fsh, fsl = fmats
    tf = min(512, ls)
    fspec = pl.BlockSpec((tf, ls), lambda j: (j, 0))
    tspec = pl.BlockSpec((ls, 2 * GROUP_W), lambda j: (0, 0))
    ospec = pl.BlockSpec((tf, 2 * GROUP_W), lambda j: (j, 0))
    kre, kim = pl.pallas_call(
        _hyena_spec_kernel,
        grid=(ls // tf,),
        in_specs=[fspec, fspec, fspec, fspec, tspec, tspec],
        out_specs=[ospec, ospec],
        out_shape=[jax.ShapeDtypeStruct((ls, 2 * GROUP_W), F32)] * 2,
        compiler_params=_cparams(("arbitrary",)),
        name="hyena_spectra",
    )(fch, fcl, fsh, fsl, hsum, hdif)
    return kre, kim, knyq


def _hyena_kernel(p_ref, fc_ref, fs_ref, kre_ref, kim_ref, knyq_ref, cw_ref, cb_ref, d_ref, ng_ref, nb_ref,
                  o_ref, ub_ref, yre_ref, yim_ref, z_ref, nyq_ref):
    ls = p_ref.shape[1]
    tf = min(512, ls)
    nt = ls // tf
    g = GROUP_W
    ones_bd = _head_ones()

    def alt_sign(s):
        rows = _iota((tf, 1), 0) + s
        return 1.0 - 2.0 * (rows & 1).astype(F32)

    def long_conv(order):
        c0 = order * g

        def fwd(j, carry):
            s = pl.multiple_of(j * tf, tf)
            ure = _dg(fc_ref[pl.ds(s, tf), :], ub_ref[...])
            uim = _dg(fs_ref[pl.ds(s, tf), :], ub_ref[...])
            kre = kre_ref[pl.ds(s, tf), c0:c0 + g]
            kim = kim_ref[pl.ds(s, tf), c0:c0 + g]
            yre_ref[pl.ds(s, tf), :] = _bf(ure * kre - uim * kim)
            yim_ref[pl.ds(s, tf), :] = _bf(ure * kim + uim * kre)
            return carry

        lax.fori_loop(0, nt, fwd, 0)

    def inv_tile(s, order):
        c0 = order * g
        y = _dg(fc_ref[pl.ds(s, tf), :], yre_ref[...]) + _dg(fs_ref[pl.ds(s, tf), :], yim_ref[...])
        return y + alt_sign(s) * (nyq_ref[...] * knyq_ref[0:1, c0:c0 + g])

    def prep(j, acc):
        s = pl.multiple_of(j * tf, tf)
        v = _dwconv_rows(p_ref, s, tf, ls, 0, g, cw_ref, cb_ref)
        z_ref[pl.ds(s, tf), :] = v
        ub_ref[pl.ds(s, tf), :] = _bf(v)
        return acc + jnp.sum(v * alt_sign(s), axis=0, keepdims=True)

    nyq_ref[...] = lax.fori_loop(0, nt, prep, jnp.zeros((1, g), F32))
    long_conv(0)

    def mid(j, acc):
        s = pl.multiple_of(j * tf, tf)
        v = z_ref[pl.ds(s, tf), :]
        x1 = _dwconv_rows(p_ref, s, tf, ls, g, 2 * g, cw_ref, cb_ref)
        z2 = x1 * (inv_tile(s, 0) + v * d_ref[0:1, :])
        z_ref[pl.ds(s, tf), :] = z2
        ub_ref[pl.ds(s, tf), :] = _bf(z2)
        return acc + jnp.sum(z2 * alt_sign(s), axis=0, keepdims=True)

    nyq2 = lax.fori_loop(0, nt, mid, jnp.zeros((1, g), F32))
    nyq_ref[...] = nyq2
    long_conv(1)

    def fin(j, carry):
        s = pl.multiple_of(j * tf, tf)
        z2 = z_ref[pl.ds(s, tf), :]
        x2 = _dwconv_rows(p_ref, s, tf, ls, 2 * g, 3 * g, cw_ref, cb_ref)
        y = x2 * (inv_tile(s, 1) + z2 * d_ref[1:2, :])
        o_ref[0, pl.ds(s, tf), :] = _head_norm(y, ones_bd, ng_ref[...], nb_ref[...]).astype(o_ref.dtype)
        return carry

    lax.fori_loop(0, nt, fin, 0)


def _hyena_call(p, fc, fs, kre, kim, knyq, conv_w, conv_b, d_bias, ng, nb):
    bsz, ls, w = p.shape
    consts = [fc, fs, kre, kim, knyq, conv_w, conv_b.reshape(1, -1), d_bias, ng.reshape(1, -1), nb.reshape(1, -1)]
    return pl.pallas_call(
        _hyena_kernel,
        grid=(bsz,),
        in_specs=[_seq_spec(ls, w)] + [_const_spec(a.shape) for a in consts],
        out_specs=_seq_spec(ls, GROUP_W),
        out_shape=jax.ShapeDtypeStruct((bsz, ls, GROUP_W), ACT),
        scratch_shapes=[pltpu.VMEM((ls, GROUP_W), BF16), pltpu.VMEM((ls, GROUP_W), BF16),
                        pltpu.VMEM((ls, GROUP_W), BF16), pltpu.VMEM((ls, GROUP_W), F32),
                        pltpu.VMEM((1, GROUP_W), F32)],
        compiler_params=_cparams(("parallel",)),
        name="hyena_mixer",
    )(p, *consts)


def _outproj_kernel(ya, yb, yr, yw, x_ref, gate_ref, w_ref, g_ref, b_ref, o_ref, *, alpha):
    y = None
    for i, part in enumerate((ya, yb, yr, yw)):
        c = _dg(_bf(part[0]), w_ref[i * GROUP_W:(i + 1) * GROUP_W, :])
        y = c if y is None else y + c
    xn = alpha * x_ref[0] + gate_ref[0] * y
    o_ref[0] = _ln_rows(xn) * g_ref[...] + b_ref[...]


def _outproj_call(parts, x, gate, w_bf, g, b, alpha):
    bsz, ls, d = x.shape
    tm = min(512, ls)
    row = lambda w: pl.BlockSpec((1, tm, w), lambda bb, i: (bb, i, 0))
    consts = [w_bf, g.reshape(1, -1), b.reshape(1, -1)]
    return pl.pallas_call(
        functools.partial(_outproj_kernel, alpha=alpha),
        grid=(bsz, ls // tm),
        in_specs=[row(GROUP_W)] * 4 + [row(d), _mod_spec(gate)] + [_const_spec(a.shape) for a in consts],
        out_specs=row(d),
        out_shape=jax.ShapeDtypeStruct((bsz, ls, d), F32),
        compiler_params=_cparams(("parallel", "parallel")),
        name="out_proj_ln",
    )(*parts, x, gate, *consts)


def _ffn_kernel(xp_ref, x_ref, xn_ref, sh_ref, sc_ref, gate_ref, wa_ref, wb_ref, cw_ref, cb_ref, wd_ref,
                g_ref, b_ref, o_ref, *, alpha):
    i = pl.program_id(1)
    nt = pl.num_programs(1)
    tl = x_ref.shape[1]
    h8 = SUBLANES
    xm = x_ref[0]
    xa = jnp.concatenate([xp_ref[0], xm, xn_ref[0]], axis=0)
    u = _bf(_ln_rows(xa) * (1.0 + sc_ref[0]) + sh_ref[0])
    rows = _iota((tl + 2 * h8, 1), 0)
    valid = jnp.logical_and(jnp.logical_or(rows >= h8, i > 0), jnp.logical_or(rows < tl + h8, i < nt - 1))
    um = u[h8:tl + h8]
    nj = wa_ref.shape[0]

    def body(j, acc):
        a = jnp.where(valid, _dg(u, wa_ref[j]), 0.0)
        bb = _dg(um, wb_ref[j])
        cw = cw_ref[j]
        prev = pltpu.roll(a, 1, 0)[h8:tl + h8]
        nxt = pltpu.roll(a, tl + 2 * h8 - 1, 0)[h8:tl + h8]
        ac = prev * cw[0:1] + a[h8:tl + h8] * cw[1:2] + nxt * cw[2:3] + cb_ref[j]
        hid = ac * _sigmoid(ac) * bb
        return acc + _dg(_bf(hid), wd_ref[j])

    f = lax.fori_loop(0, nj, body, jnp.zeros(xm.shape, F32), unroll=True)
    xn = alpha * xm + gate_ref[0] * f
    o_ref[0] = _ln_rows(xn) * g_ref[...] + b_ref[...]


def _ffn_call(x, shift, scale, gate, wa, wb, cw, cb, wd, g, b, alpha):
    bsz, ls, d = x.shape
    tl = min(FFN_ROWS, ls)
    nb8 = tl // SUBLANES
    last8 = ls // SUBLANES - 1
    consts = [wa, wb, cw, cb, wd, g.reshape(1, -1), b.reshape(1, -1)]
    return pl.pallas_call(
        functools.partial(_ffn_kernel, alpha=alpha),
        grid=(bsz, ls // tl),
        in_specs=[pl.BlockSpec((1, SUBLANES, d), lambda bb, i: (bb, jnp.maximum(i * nb8 - 1, 0), 0)),
                  pl.BlockSpec((1, tl, d), lambda bb, i: (bb, i, 0)),
                  pl.BlockSpec((1, SUBLANES, d), lambda bb, i: (bb, jnp.minimum((i + 1) * nb8, last8), 0)),
                  _mod_spec(shift), _mod_spec(scale), _mod_spec(gate)] + [_const_spec(a.shape) for a in consts],
        out_specs=pl.BlockSpec((1, tl, d), lambda bb, i: (bb, i, 0)),
        out_shape=jax.ShapeDtypeStruct((bsz, ls, d), F32),
        compiler_params=_cparams(("parallel", "parallel")),
        name="conv_ffn_ln",
    )(x, x, x, shift, scale, gate, *consts)


FFN_COLS = 256


def _prep_w_in(w):
    d = w.shape[0]
    m = 4 * GROUP_W + 4 * N_HEADS_G
    pad = jnp.zeros((d, SEG_W[0] - m), w.dtype)
    return _bf(jnp.concatenate([w[:, :m], pad, w[:, m:]], axis=1))


def _prep_ffn(w_up, conv_w, conv_b, w_down):
    d, two_ff = w_up.shape
    dff = two_ff // 2
    nj = dff // FFN_COLS
    wa = _bf(w_up[:, :dff]).reshape(d, nj, FFN_COLS).transpose(1, 0, 2)
    wb = _bf(w_up[:, dff:]).reshape(d, nj, FFN_COLS).transpose(1, 0, 2)
    cw = conv_w.reshape(3, nj, FFN_COLS).transpose(1, 0, 2)
    cb = conv_b.reshape(nj, 1, FFN_COLS)
    wd = _bf(w_down).reshape(nj, FFN_COLS, d)
    return wa, wb, cw, cb, wd


@functools.lru_cache(maxsize=None)
def _dft_inputs(ls):
    out = []
    for m in _dft_mats(ls):
        hi = m.astype(np.float32).astype(BF16)
        lo = (m - hi.astype(np.float64)).astype(np.float32).astype(BF16)
        out += [hi, lo]
    return tuple(out)


def kernel(x, c, ctx, c_ctx, ada_w, ada_b, w_in, mlstm_conv_w, mlstm_conv_b, mlstm_gate_b, hyena_conv_w, hyena_conv_b, hyena_w1, hyena_b1, hyena_w2, hyena_b2, hyena_w3, hyena_freq, hyena_d, rwkv_mu, rwkv_w0, rwkv_w2, rwkv_a0, rwkv_a2, rwkv_g2, rwkv_kk, rwkv_ka, rwkv_rk, out_norm_g, out_norm_b, w_out, ln1_g, ln1_b, ffn_w_up, ffn_conv_w, ffn_conv_b, ffn_w_down, ln2_g, ln2_b):
    bsz, ls, d = x.shape
    lc = ctx.shape[1]
    depth = ada_w.shape[0]
    alpha = float((2 * depth) ** 0.25)
    g = GROUP_W

    n_rows = -(-(bsz + 1) // SUBLANES) * SUBLANES
    cc = jnp.concatenate([c, c_ctx[None, :], jnp.zeros((n_rows - bsz - 1, d), F32)], axis=0)

    dft_l = _dft_inputs(ls)
    dft_c = _dft_inputs(lc)

    xc = ctx
    for l in range(depth):
        last = l == depth - 1
        mod_all = _ada_call(cc, ada_w[l], ada_b[l])
        mod = [mod_all[:bsz, i * d:(i + 1) * d].reshape(bsz, 1, d) for i in range(6)]
        modc = [mod_all[bsz:bsz + 1, i * d:(i + 1) * d].reshape(1, 1, d) for i in range(6)]
        w_in_bf = _prep_w_in(w_in[l])
        ng, nb = out_norm_g[l], out_norm_b[l]
        gs = lambda i, t: t[i * g:(i + 1) * g]

        pa, ph, pr, pw = _inproj_call(x, mod[0], mod[1], w_in_bf)
        pa_c, ph_c, pr_c, pw_c = _inproj_call(xc, modc[0], modc[1], w_in_bf)

        ya, ya_c = _mlstm_call(pa, pa_c, mlstm_conv_w[l], mlstm_conv_b[l], mlstm_gate_b[l], gs(0, ng), gs(0, nb))
        hy = (hyena_w1[l], hyena_b1[l], hyena_w2[l], hyena_b2[l], hyena_w3[l], hyena_freq[l])
        kre, kim, knyq = _hyena_filters(ls, *hy, dft_l)
        yb = _hyena_call(ph, dft_l[0], dft_l[2], kre, kim, knyq, hyena_conv_w[l], hyena_conv_b[l], hyena_d[l],
                         gs(1, ng), gs(1, nb))
        yr, yr_c = _ret_call(pr, pr_c, gs(2, ng), gs(2, nb))
        yw, yw_c = _rwkv_call(pw, pw_c, rwkv_mu[l], rwkv_w0[l], rwkv_w2[l], rwkv_a0[l], rwkv_a2[l], rwkv_g2[l],
                              rwkv_kk[l], rwkv_ka[l], rwkv_rk[l], gs(3, ng), gs(3, nb))

        w_out_bf = _bf(w_out[l])
        ffn_w = _prep_ffn(ffn_w_up[l], ffn_conv_w[l], ffn_conv_b[l], ffn_w_down[l])
        x = _outproj_call((ya, yb, yr, yw), x, mod[2], w_out_bf, ln1_g[l], ln1_b[l], alpha)
        x = _ffn_call(x, mod[3], mod[4], mod[5], *ffn_w, ln2_g[l], ln2_b[l], alpha)
        if not last:
            kre_c, kim_c, knyq_c = _hyena_filters(lc, *hy, dft_c)
            yb_c = _hyena_call(ph_c, dft_c[0], dft_c[2], kre_c, kim_c, knyq_c, hyena_conv_w[l], hyena_conv_b[l],
                               hyena_d[l], gs(1, ng), gs(1, nb))
            xc = _outproj_call((ya_c, yb_c, yr_c, yw_c), xc, modc[2], w_out_bf, ln1_g[l], ln1_b[l], alpha)
            xc = _ffn_call(xc, modc[3], modc[4], modc[5], *ffn_w, ln2_g[l], ln2_b[l], alpha)
    return x
```
